```python
import jax, jax.numpy as jnp
from jax import lax
import numpy as np

D_MODEL = 2048
BATCH = 4
SEQ = 4096
DEPTH = 2

RMS_EPS = 1e-6
CONV_WIDTH = 4
LRU_WIDTH = D_MODEL
LRU_BLOCKS = 16
LRU_BLOCK = LRU_WIDTH // LRU_BLOCKS
LRU_C = 8.0
MLSTM_WIDTH = D_MODEL
MLSTM_HEADS = 8
MLSTM_HEAD_DIM = MLSTM_WIDTH // MLSTM_HEADS
MLSTM_QKV_BLOCK = 4
MLSTM_QKV_BLOCKS = MLSTM_WIDTH // MLSTM_QKV_BLOCK
MLSTM_CHUNK = 128
MLSTM_GN_EPS = 1e-6
EVEN_MIX = LRU_WIDTH + MLSTM_WIDTH
RWKV_WIDTH = 2 * D_MODEL
RWKV_HEAD_DIM = 64
RWKV_HEADS = RWKV_WIDTH // RWKV_HEAD_DIM
RWKV_DECAY_RANK = 96
RWKV_A_RANK = 96
RWKV_GN_EPS = 64e-5

kernel_name = 'hybrid_rglru_mlstm_rwkv7'


def rms_norm(x, g):
    xf = x.astype(jnp.float32)
    y = xf * lax.rsqrt(jnp.mean(xf * xf, axis=-1, keepdims=True) + RMS_EPS)
    return (y * g.astype(jnp.float32)).astype(x.dtype)


def head_norm(x, n_heads, eps, g, b=None):
    xf = x.astype(jnp.float32)
    xh = xf.reshape(*xf.shape[:-1], n_heads, -1)
    mu = jnp.mean(xh, axis=-1, keepdims=True)
    var = jnp.mean(jnp.square(xh - mu), axis=-1, keepdims=True)
    y = ((xh - mu) * lax.rsqrt(var + eps)).reshape(xf.shape) * g.astype(jnp.float32)
    if b is not None:
        y = y + b.astype(jnp.float32)
    return y.astype(x.dtype)


def shift_right(x):
    return jnp.pad(x[:, :-1], ((0, 0), (1, 0), (0, 0)))


def causal_dwconv(x, w, b):
    K = w.shape[0]
    S = x.shape[1]
    xp = jnp.pad(x, ((0, 0), (K - 1, 0), (0, 0)))
    y = b
    for j in range(K):
        y = y + xp[:, j:j + S] * w[j]
    return y


def block_diag(x, w):
    nb, bs, bo = w.shape
    xb = x.reshape(*x.shape[:-1], nb, bs)
    return jnp.einsum('bsni,nij->bsnj', xb, w).reshape(x.shape[:-1] + (nb * bo,))


def rg_lru(x, w_a, b_a, w_x, b_x, lam):
    xf = x.astype(jnp.float32)
    r = jax.nn.sigmoid(block_diag(xf, w_a.astype(jnp.float32)) + b_a.astype(jnp.float32))
    i = jax.nn.sigmoid(block_diag(xf, w_x.astype(jnp.float32)) + b_x.astype(jnp.float32))
    log_a = -LRU_C * r * jax.nn.softplus(-lam.astype(jnp.float32))
    a = jnp.exp(log_a)
    u = jnp.sqrt(-jnp.expm1(2.0 * log_a)) * (i * xf)

    def combine(c1, c2):
        a1, b1 = c1
        a2, b2 = c2
        return a1 * a2, a2 * b1 + b2

    _, h = lax.associative_scan(combine, (a, u), axis=1)
    return h.astype(x.dtype)


def mlstm_chunkwise(q, k, v, ig, fg):
    B, S, _ = q.shape
    H, d, L = MLSTM_HEADS, MLSTM_HEAD_DIM, MLSTM_CHUNK
    nc = S // L

    def to_chunks(t):
        return t.astype(jnp.float32).reshape(B, nc, L, H, d).transpose(1, 0, 3, 2, 4)

    def gate_chunks(t):
        return t.astype(jnp.float32).reshape(B, nc, L, H).transpose(1, 0, 3, 2)

    qc = to_chunks(q)
    kc = to_chunks(k) * (d ** -0.5)
    vc = to_chunks(v)
    igc = gate_chunks(ig)
    lfc = jax.nn.log_sigmoid(gate_chunks(fg))
    causal = jnp.tril(jnp.ones((L, L), dtype=bool))

    def body(carry, inp):
        C, n, m = carry
        qb, kb, vb, ib, lb = inp
        b = jnp.cumsum(lb, axis=-1)
        Dm = b[..., :, None] - b[..., None, :] + ib[..., None, :]
        Dm = jnp.where(causal, Dm, -jnp.inf)
        m_inter = b + m[..., None]
        m_t = jnp.maximum(jnp.max(Dm, axis=-1), m_inter)
        scores = jnp.einsum('bhtd,bhsd->bhts', qb, kb) * jnp.exp(Dm - m_t[..., None])
        inter = jnp.exp(m_inter - m_t)
        num = (jnp.einsum('bhts,bhsd->bhtd', scores, vb)
               + inter[..., None] * jnp.einsum('bhtd,bhde->bhte', qb, C))
        den = jnp.sum(scores, axis=-1) + inter * jnp.einsum('bhtd,bhd->bht', qb, n)
        h = num / jnp.maximum(jnp.abs(den), jnp.exp(-m_t))[..., None]
        bL = b[..., -1]
        g = bL[..., None] - b + ib
        m_new = jnp.maximum(bL + m, jnp.max(g, axis=-1))
        decay = jnp.exp(bL + m - m_new)
        wts = jnp.exp(g - m_new[..., None])
        C = decay[..., None, None] * C + jnp.einsum('bhs,bhsd,bhse->bhde', wts, kb, vb)
        n = decay[..., None] * n + jnp.einsum('bhs,bhsd->bhd', wts, kb)
        return (C, n, m_new), h

    init = (jnp.zeros((B, H, d, d), jnp.float32),
            jnp.zeros((B, H, d), jnp.float32),
            jnp.zeros((B, H), jnp.float32))
    _, h = lax.scan(body, init, (qc, kc, vc, igc, lfc))
    return h.transpose(1, 0, 3, 2, 4).reshape(B, S, H * d).astype(q.dtype)


def rwkv7_scan(r, w, k, v, kk, a):
    B, _, H, d = r.shape

    def step(S, inp):
        r_t, w_t, k_t, v_t, kk_t, a_t = inp
        sa = jnp.einsum('bhvk,bhk->bhv', S, -kk_t)
        S = (S * w_t[:, :, None, :] + sa[..., None] * (kk_t * a_t)[:, :, None, :]
             + v_t[..., None] * k_t[:, :, None, :])
        return S, jnp.einsum('bhvk,bhk->bhv', S, r_t)

    xs = (jnp.moveaxis(r, 1, 0), jnp.moveaxis(w, 1, 0), jnp.moveaxis(k, 1, 0),
          jnp.moveaxis(v, 1, 0), jnp.moveaxis(kk, 1, 0), jnp.moveaxis(a, 1, 0))
    _, y = lax.scan(step, jnp.zeros((B, H, d, d), jnp.float32), xs)
    return jnp.moveaxis(y, 0, 1)


def even_mixer(h, w_in, lru_conv_w, lru_conv_b, lru_wa, lru_ba, lru_wx, lru_bx, lru_lambda,
               m_conv_w, m_conv_b, m_wq, m_wk, m_wv, m_wi, m_bi, m_wf, m_bf, m_skip, m_gn,
               w_out):
    u = h @ w_in
    xr, zr, xm, zm = jnp.split(
        u, [LRU_WIDTH, 2 * LRU_WIDTH, 2 * LRU_WIDTH + MLSTM_WIDTH], axis=-1)
    xr = causal_dwconv(xr, lru_conv_w, lru_conv_b)
    yr = rg_lru(xr, lru_wa, lru_ba, lru_wx, lru_bx, lru_lambda)
    xmc = jax.nn.silu(causal_dwconv(xm, m_conv_w, m_conv_b))
    q = block_diag(xmc, m_wq)
    k = block_diag(xmc, m_wk)
    v = block_diag(xm, m_wv)
    qkv = jnp.concatenate([q, k, v], axis=-1)
    ig = qkv @ m_wi + m_bi
    fg = qkv @ m_wf + m_bf
    ym = mlstm_chunkwise(q, k, v, ig, fg)
    ym = head_norm(ym, MLSTM_HEADS, MLSTM_GN_EPS, m_gn) + m_skip * xmc
    y = jnp.concatenate([yr * jax.nn.silu(zr), ym * jax.nn.silu(zm)], axis=-1)
    return y @ w_out


def odd_mixer(h, w_in, mu_rkv, mu_w, mu_a, w0, w1, w2, a0, a1, a2, k_k, k_a, r_k,
              gn_g, gn_b, w_out):
    B, S, _ = h.shape
    H, d = RWKV_HEADS, RWKV_HEAD_DIM
    u = h @ w_in
    rkv, z = u[..., :3 * RWKV_WIDTH], u[..., 3 * RWKV_WIDTH:]
    rkv = rkv + (shift_right(rkv) - rkv) * mu_rkv
    r, k, v = jnp.split(rkv, 3, axis=-1)
    dh = shift_right(h) - h
    xw = h + dh * mu_w
    xa = h + dh * mu_a
    w_log = -jax.nn.softplus(-(w0 + jnp.tanh(xw @ w1) @ w2)) - 0.5
    decay = jnp.exp(-jnp.exp(w_log.astype(jnp.float32)))
    a = jax.nn.sigmoid((a0 + (xa @ a1) @ a2).astype(jnp.float32))
    kk = (k * k_k).astype(jnp.float32).reshape(B, S, H, d)
    kk = kk / jnp.maximum(jnp.sqrt(jnp.sum(kk * kk, axis=-1, keepdims=True)), 1e-12)
    k = k.astype(jnp.float32) * (1.0 + (a - 1.0) * k_a.astype(jnp.float32))
    rh = r.astype(jnp.float32).reshape(B, S, H, d)
    kh = k.reshape(B, S, H, d)
    vh = v.astype(jnp.float32).reshape(B, S, H, d)
    ah = a.reshape(B, S, H, d)
    wh = decay.reshape(B, S, H, d)
    y = rwkv7_scan(rh, wh, kh, vh, kk, ah)
    y = head_norm(y.reshape(B, S, RWKV_WIDTH), H, RWKV_GN_EPS, gn_g, gn_b)
    bonus = jnp.sum(rh * kh * r_k.astype(jnp.float32), axis=-1, keepdims=True) * vh
    y = (y + bonus.reshape(B, S, RWKV_WIDTH)).astype(h.dtype)
    return (y * jax.nn.silu(z)) @ w_out


def setup_inputs(seed: int = 0) -> dict:
    key = jax.random.key(seed)
    ks = iter(jax.random.split(key, 64))
    f32 = jnp.float32

    def nrm(shape, scale):
        return jax.random.normal(next(ks), shape, f32) * scale

    def gain(n):
        return 1.0 + nrm((n,), 0.02)

    def unif(shape, lo, hi):
        return jax.random.uniform(next(ks), shape, f32, lo, hi)

    D, R, M, W = D_MODEL, LRU_WIDTH, MLSTM_WIDTH, RWKV_WIDTH
    x = jax.random.normal(next(ks), (BATCH, SEQ, D), f32)
    s = unif((R,), 0.9, 0.999) ** (1.0 / LRU_C)
    lru_lambda = jnp.log(s) - jnp.log1p(-s)
    w0 = jnp.tile(jnp.linspace(-6.0, -1.0, RWKV_HEAD_DIM, dtype=f32), RWKV_HEADS) + nrm((W,), 0.1)
    return {
        'x': x,
        'l0_norm_pre': gain(D),
        'l0_w_in': nrm((D, 2 * EVEN_MIX), D ** -0.5),
        'l0_lru_conv_w': nrm((CONV_WIDTH, R), CONV_WIDTH ** -0.5),
        'l0_lru_conv_b': nrm((R,), 0.01),
        'l0_lru_wa': nrm((LRU_BLOCKS, LRU_BLOCK, LRU_BLOCK), LRU_BLOCK ** -0.5),
        'l0_lru_ba': nrm((R,), 0.01),
        'l0_lru_wx': nrm((LRU_BLOCKS, LRU_BLOCK, LRU_BLOCK), LRU_BLOCK ** -0.5),
        'l0_lru_bx': nrm((R,), 0.01),
        'l0_lru_lambda': lru_lambda,
        'l0_m_conv_w': nrm((CONV_WIDTH, M), CONV_WIDTH ** -0.5),
        'l0_m_conv_b': nrm((M,), 0.01),
        'l0_m_wq': nrm((MLSTM_QKV_BLOCKS, MLSTM_QKV_BLOCK, MLSTM_QKV_BLOCK), MLSTM_QKV_BLOCK ** -0.5),
        'l0_m_wk': nrm((MLSTM_QKV_BLOCKS, MLSTM_QKV_BLOCK, MLSTM_QKV_BLOCK), MLSTM_QKV_BLOCK ** -0.5),
        'l0_m_wv': nrm((MLSTM_QKV_BLOCKS, MLSTM_QKV_BLOCK, MLSTM_QKV_BLOCK), MLSTM_QKV_BLOCK ** -0.5),
        'l0_m_wi': nrm((3 * M, MLSTM_HEADS), (3 * M) ** -0.5),
        'l0_m_bi': nrm((MLSTM_HEADS,), 0.1),
        'l0_m_wf': nrm((3 * M, MLSTM_HEADS), (3 * M) ** -0.5),
        'l0_m_bf': jnp.linspace(3.0, 6.0, MLSTM_HEADS, dtype=f32) + nrm((MLSTM_HEADS,), 0.1),
        'l0_m_skip': gain(M),
        'l0_m_gn': gain(M),
        'l0_w_out': nrm((EVEN_MIX, D), EVEN_MIX ** -0.5),
        'l0_norm_post': gain(D),
        'l1_norm_pre': gain(D),
        'l1_w_in': nrm((D, 4 * W), D ** -0.5),
        'l1_mu_rkv': unif((3 * W,), 0.0, 1.0),
        'l1_mu_w': unif((D,), 0.0, 1.0),
        'l1_mu_a': unif((D,), 0.0, 1.0),
        'l1_w0': w0,
        'l1_w1': nrm((D, RWKV_DECAY_RANK), D ** -0.5),
        'l1_w2': nrm((RWKV_DECAY_RANK, W), 0.1 * RWKV_DECAY_RANK ** -0.5),
        'l1_a0': nrm((W,), 0.1),
        'l1_a1': nrm((D, RWKV_A_RANK), D ** -0.5),
        'l1_a2': nrm((RWKV_A_RANK, W), RWKV_A_RANK ** -0.5),
        'l1_k_k': 0.85 + nrm((W,), 0.02),
        'l1_k_a': 1.0 + nrm((W,), 0.02),
        'l1_r_k': nrm((RWKV_HEADS, RWKV_HEAD_DIM), 0.1),
        'l1_gn_g': gain(W),
        'l1_gn_b': nrm((W,), 0.02),
        'l1_w_out': nrm((W, D), W ** -0.5),
        'l1_norm_post': gain(D),
    }


def reference(x, l0_norm_pre, l0_w_in, l0_lru_conv_w, l0_lru_conv_b, l0_lru_wa, l0_lru_ba,
              l0_lru_wx, l0_lru_bx, l0_lru_lambda, l0_m_conv_w, l0_m_conv_b, l0_m_wq, l0_m_wk,
              l0_m_wv, l0_m_wi, l0_m_bi, l0_m_wf, l0_m_bf, l0_m_skip, l0_m_gn, l0_w_out,
              l0_norm_post, l1_norm_pre, l1_w_in, l1_mu_rkv, l1_mu_w, l1_mu_a, l1_w0, l1_w1,
              l1_w2, l1_a0, l1_a1, l1_a2, l1_k_k, l1_k_a, l1_r_k, l1_gn_g, l1_gn_b, l1_w_out,
              l1_norm_post):
    even_p = (l0_w_in, l0_lru_conv_w, l0_lru_conv_b, l0_lru_wa, l0_lru_ba, l0_lru_wx,
              l0_lru_bx, l0_lru_lambda, l0_m_conv_w, l0_m_conv_b, l0_m_wq, l0_m_wk, l0_m_wv,
              l0_m_wi, l0_m_bi, l0_m_wf, l0_m_bf, l0_m_skip, l0_m_gn, l0_w_out)
    odd_p = (l1_w_in, l1_mu_rkv, l1_mu_w, l1_mu_a, l1_w0, l1_w1, l1_w2, l1_a0, l1_a1, l1_a2,
             l1_k_k, l1_k_a, l1_r_k, l1_gn_g, l1_gn_b, l1_w_out)
    layer_params = [(l0_norm_pre, l0_norm_post, even_p), (l1_norm_pre, l1_norm_post, odd_p)]
    for layer in range(DEPTH):
        g_pre, g_post, p = layer_params[layer]
        h = rms_norm(x, g_pre)
        h = even_mixer(h, *p) if layer % 2 == 0 else odd_mixer(h, *p)
        x = x + rms_norm(h, g_post)
    return x
```

```python
import functools

import jax
import jax.numpy as jnp
from jax import lax
from jax.experimental import pallas as pl
from jax.experimental.pallas import tpu as pltpu

F32 = jnp.float32
BF16 = jnp.bfloat16

RMS_EPS = 1e-6
CONV_WIDTH = 4
LRU_C = 8.0
LRU_BLOCK = 128
MLSTM_HEADS = 8
MLSTM_QKV_BLOCK = 4
MLSTM_CHUNK = 128
MLSTM_GN_EPS = 1e-6
RWKV_HEAD_DIM = 64
RWKV_GN_EPS = 64e-5
RWKV_CHUNK = 64
RWKV_HEADS_PER_STEP = 8
LORA_PAD = 128

LANES = 128
SUBLANES = 8
VMEM_LIMIT = 56 * 1024 * 1024


def _params(sem):
    return pltpu.CompilerParams(dimension_semantics=sem, vmem_limit_bytes=VMEM_LIMIT)


def _sigmoid(x):
    return 1.0 / (1.0 + jnp.exp(-x))


def _silu(x):
    return x * _sigmoid(x)


def _softplus(x):
    return jnp.maximum(x, 0.0) + jnp.log1p(jnp.exp(-jnp.abs(x)))


def _dot(a, b):
    return jnp.dot(a.astype(BF16), b.astype(BF16), preferred_element_type=F32)


def _dot_nt(a, b):
    return lax.dot_general(a.astype(BF16), b.astype(BF16), (((1,), (1,)), ((), ())),
                           preferred_element_type=F32)


def _dot_tn(a, b):
    return lax.dot_general(a.astype(BF16), b.astype(BF16), (((0,), (0,)), ((), ())),
                           preferred_element_type=F32)


def _split3(x):
    hi = x.astype(BF16)
    r1 = x - hi.astype(F32)
    mid = r1.astype(BF16)
    lo = (r1 - mid.astype(F32)).astype(BF16)
    return hi, mid, lo


def _dot_exact_rhs(a_f32, b_bf16):
    hi, mid, lo = _split3(a_f32)
    out = jnp.dot(hi, b_bf16, preferred_element_type=F32)
    out = out + jnp.dot(mid, b_bf16, preferred_element_type=F32)
    return out + jnp.dot(lo, b_bf16, preferred_element_type=F32)


def _dot_exact_lhs(a_bf16, b_f32):
    hi, mid, lo = _split3(b_f32)
    out = jnp.dot(a_bf16, hi, preferred_element_type=F32)
    out = out + jnp.dot(a_bf16, mid, preferred_element_type=F32)
    return out + jnp.dot(a_bf16, lo, preferred_element_type=F32)


def _rms_norm(x, g):
    return x * lax.rsqrt(jnp.mean(x * x, axis=-1, keepdims=True) + RMS_EPS) * g


def _tri(n, inclusive=True):
    r = lax.broadcasted_iota(jnp.int32, (n, n), 0)
    c = lax.broadcasted_iota(jnp.int32, (n, n), 1)
    return (c <= r) if inclusive else (c < r)


def _l0_in_kernel(x_ref, g_ref, w_ref, of_ref, oz_ref, h_ref, *, nf):
    j = pl.program_id(1)

    @pl.when(j == 0)
    def _():
        h_ref[...] = _rms_norm(x_ref[...], g_ref[...]).astype(BF16)

    acc = jnp.dot(h_ref[...], w_ref[...], preferred_element_type=F32)

    @pl.when(j < nf)
    def _():
        of_ref[...] = acc

    @pl.when(j >= nf)
    def _():
        oz_ref[...] = acc.astype(BF16)


def _l0_in(x2, g, w_bf16, n_f32, tm=512, tn=1024):
    t, d = x2.shape
    n = w_bf16.shape[1]
    nf = n_f32 // tn
    nz = (n - n_f32) // tn
    return pl.pallas_call(
        functools.partial(_l0_in_kernel, nf=nf),
        grid=(t // tm, nf + nz),
        in_specs=[pl.BlockSpec((tm, d), lambda i, j: (i, 0)),
                  pl.BlockSpec((1, d), lambda i, j: (0, 0)),
                  pl.BlockSpec((d, tn), lambda i, j: (0, j))],
        out_specs=[pl.BlockSpec((tm, tn), lambda i, j: (i, jnp.minimum(j, nf - 1))),
                   pl.BlockSpec((tm, tn), lambda i, j: (i, jnp.maximum(j - nf, 0)))],
        out_shape=[jax.ShapeDtypeStruct((t, n_f32), F32),
                   jax.ShapeDtypeStruct((t, n - n_f32), BF16)],
        scratch_shapes=[pltpu.VMEM((tm, d), BF16)],
        compiler_params=_params(("arbitrary", "arbitrary")),
        name="l0_in",
    )(x2, g, w_bf16)


def _conv_from_buf(xbuf, ts, w, b):
    y = b
    for j in range(CONV_WIDTH):
        off = SUBLANES - (CONV_WIDTH - 1) + j
        y = y + xbuf[pl.ds(off, ts), :] * w[j:j + 1, :]
    return y


def _lru_kernel(x_ref, z_ref, cw_ref, cb_ref, wa_ref, ba_ref, wx_ref, bx_ref, lam_ref, o_ref,
                xbuf, a_s, u_s, hc_s, *, ts, nblk):
    s = pl.program_id(2)

    @pl.when(s == 0)
    def _():
        xbuf[pl.ds(0, SUBLANES), :] = jnp.zeros((SUBLANES, xbuf.shape[1]), F32)
        hc_s[...] = jnp.zeros_like(hc_s)

    xbuf[pl.ds(SUBLANES, ts), :] = x_ref[...]
    xc = _conv_from_buf(xbuf, ts, cw_ref[...], cb_ref[...])
    xbuf[pl.ds(0, SUBLANES), :] = xbuf[pl.ds(ts, SUBLANES), :]

    gr, gi = [], []
    for n in range(nblk):
        xn = xc[:, n * LRU_BLOCK:(n + 1) * LRU_BLOCK]
        gr.append(_dot(xn, wa_ref[n]))
        gi.append(_dot(xn, wx_ref[n]))
    r = _sigmoid(jnp.concatenate(gr, axis=1) + ba_ref[...])
    i = _sigmoid(jnp.concatenate(gi, axis=1) + bx_ref[...])
    log_a = -LRU_C * r * _softplus(-lam_ref[...])
    a = jnp.exp(log_a)
    a_s[...] = a
    u_s[...] = jnp.sqrt(-jnp.tanh(log_a) * (a * a + 1.0)) * (i * xc)

    row = lax.broadcasted_iota(jnp.int32, (SUBLANES, a_s.shape[1]), 0)

    def body(g, carry):
        r0 = pl.multiple_of(g * SUBLANES, SUBLANES)
        a = a_s[pl.ds(r0, SUBLANES), :]
        u = u_s[pl.ds(r0, SUBLANES), :]
        for sft in (1, 2, 4):
            a_sh = jnp.where(row >= sft, pltpu.roll(a, sft, 0), 1.0)
            u_sh = jnp.where(row >= sft, pltpu.roll(u, sft, 0), 0.0)
            u = a * u_sh + u
            a = a * a_sh
        h = a * carry + u
        u_s[pl.ds(r0, SUBLANES), :] = h
        return h[SUBLANES - 1:SUBLANES, :]

    hc_s[...] = lax.fori_loop(0, ts // SUBLANES, body, hc_s[...])
    o_ref[...] = (u_s[...] * _silu(z_ref[...].astype(F32))).astype(BF16)


def _l0_lru(uf, uz, p, batch, seq, width, ts=256, tc=256):
    nblk = tc // LRU_BLOCK
    nct = width // tc
    nst = seq // ts
    vec = lambda: pl.BlockSpec((1, tc), lambda b, c, s: (0, c))
    return pl.pallas_call(
        functools.partial(_lru_kernel, ts=ts, nblk=nblk),
        grid=(batch, nct, nst),
        in_specs=[pl.BlockSpec((ts, tc), lambda b, c, s: (b * nst + s, c)),
                  pl.BlockSpec((ts, tc), lambda b, c, s: (b * nst + s, c)),
                  pl.BlockSpec((CONV_WIDTH, tc), lambda b, c, s: (0, c)),
                  vec(),
                  pl.BlockSpec((nblk, LRU_BLOCK, LRU_BLOCK), lambda b, c, s: (c, 0, 0)),
                  vec(),
                  pl.BlockSpec((nblk, LRU_BLOCK, LRU_BLOCK), lambda b, c, s: (c, 0, 0)),
                  vec(), vec()],
        out_specs=pl.BlockSpec((ts, tc), lambda b, c, s: (b * nst + s, c)),
        out_shape=jax.ShapeDtypeStruct((batch * seq, width), BF16),
        scratch_shapes=[pltpu.VMEM((ts + SUBLANES, tc), F32),
                        pltpu.VMEM((ts, tc), F32),
                        pltpu.VMEM((ts, tc), F32),
                        pltpu.VMEM((1, tc), F32)],
        compiler_params=_params(("arbitrary", "arbitrary", "arbitrary")),
        name="l0_lru",
    )(uf, uz, p["cw"], p["cb"], p["wa"], p["ba"], p["wx"], p["bx"], p["lam"])


def _bd4(x, wd_ref):
    width = x.shape[1]
    outs = []
    for cb in range(width // LANES):
        sl = slice(cb * LANES, (cb + 1) * LANES)
        xb = x[:, sl]
        acc = xb * wd_ref[3:4, sl]
        for delta in (-3, -2, -1, 1, 2, 3):
            acc = acc + pltpu.roll(xb, delta % LANES, 1) * wd_ref[3 + delta:4 + delta, sl]
        outs.append(acc)
    return jnp.concatenate(outs, axis=1)


def _mlstm_kernel(x_ref, z_ref, cw_ref, cb_ref, wq_ref, wk_ref, wv_ref, wg_ref, bg_ref,
                  skip_ref, gn_ref, o_ref, xbuf, c_s, n_s, m_s, *, hd):
    L = MLSTM_CHUNK
    H = MLSTM_HEADS
    s = pl.program_id(1)

    @pl.when(s == 0)
    def _():
        xbuf[pl.ds(0, SUBLANES), :] = jnp.zeros((SUBLANES, xbuf.shape[1]), F32)
        c_s[...] = jnp.zeros_like(c_s)
        n_s[...] = jnp.zeros_like(n_s)
        m_s[...] = jnp.zeros_like(m_s)

    x = x_ref[...]
    xbuf[pl.ds(SUBLANES, L), :] = x
    xmc = _silu(_conv_from_buf(xbuf, L, cw_ref[...], cb_ref[...]))
    xbuf[pl.ds(0, SUBLANES), :] = xbuf[pl.ds(L, SUBLANES), :]

    q = _bd4(xmc, wq_ref)
    k = _bd4(xmc, wk_ref)
    v = _bd4(x, wv_ref)

    g = _dot(q, wg_ref[0]) + _dot(k, wg_ref[1]) + _dot(v, wg_ref[2]) + bg_ref[...]
    lane = lax.broadcasted_iota(jnp.int32, (L, LANES), 1)
    lf = jnp.minimum(g, 0.0) - jnp.log1p(jnp.exp(-jnp.abs(g)))
    lf = jnp.where((lane >= H) & (lane < 2 * H), lf, 0.0)
    csum = _dot_exact_lhs(_tri(L).astype(BF16), lf)
    gb = jnp.where(lane < H, g, csum)
    gbt = gb.T

    causal = _tri(L)
    kscale = hd ** -0.5
    ys = []
    for h in range(H):
        sl = slice(h * hd, (h + 1) * hd)
        qh, kh, vh = q[:, sl], k[:, sl] * kscale, v[:, sl]
        b_col = gb[:, H + h:H + h + 1]
        i_col = gb[:, h:h + 1]
        b_row = gbt[H + h:H + h + 1, :]
        i_row = gbt[h:h + 1, :]
        m_prev = m_s[h][0:1, 0:1]
        dm = jnp.where(causal, b_col - b_row + i_row, -1e30)
        m_inter = b_col + m_prev
        m_t = jnp.maximum(jnp.max(dm, axis=-1, keepdims=True), m_inter)
        scores = _dot_nt(qh, kh) * jnp.exp(dm - m_t)
        inter = jnp.exp(m_inter - m_t)
        cmat = c_s[h]
        nvec = n_s[h]
        num = _dot(scores, vh) + inter * _dot(qh, cmat)
        den = (jnp.sum(scores, axis=-1, keepdims=True)
               + inter * jnp.sum(qh * nvec, axis=-1, keepdims=True))
        hout = num / jnp.maximum(jnp.abs(den), jnp.exp(-m_t))
        b_last = b_col[L - 1:L, :]
        g_row = b_last - b_row + i_row
        g_col = b_last - b_col + i_col
        m_new = jnp.maximum(b_last + m_prev, jnp.max(g_row, axis=-1, keepdims=True))
        decay = jnp.exp(b_last + m_prev - m_new)
        kw = kh * jnp.exp(g_col - m_new)
        c_s[h] = decay * cmat + _dot_tn(kw, vh)
        n_s[h] = decay * nvec + jnp.sum(kw, axis=0, keepdims=True)
        m_s[h] = jnp.broadcast_to(m_new, (SUBLANES, LANES))
        mu = jnp.mean(hout, axis=-1, keepdims=True)
        cen = hout - mu
        var = jnp.mean(cen * cen, axis=-1, keepdims=True)
        ys.append(cen * lax.rsqrt(var + MLSTM_GN_EPS))
    y = jnp.concatenate(ys, axis=1) * gn_ref[...] + skip_ref[...] * xmc
    o_ref[...] = (y * _silu(z_ref[...].astype(F32))).astype(BF16)


def _l0_mlstm(uf, uz, p, batch, seq, width, x_col, z_col):
    L = MLSTM_CHUNK
    nst = seq // L
    hd = width // MLSTM_HEADS
    full = lambda shape: pl.BlockSpec(shape, lambda b, s: (0,) * len(shape))
    return pl.pallas_call(
        functools.partial(_mlstm_kernel, hd=hd),
        grid=(batch, nst),
        in_specs=[pl.BlockSpec((L, width), lambda b, s: (b * nst + s, x_col)),
                  pl.BlockSpec((L, width), lambda b, s: (b * nst + s, z_col)),
                  full((CONV_WIDTH, width)), full((1, width)),
                  full((SUBLANES, width)), full((SUBLANES, width)), full((SUBLANES, width)),
                  full((3, width, LANES)), full((1, LANES)),
                  full((1, width)), full((1, width))],
        out_specs=pl.BlockSpec((L, width), lambda b, s: (b * nst + s, 0)),
        out_shape=jax.ShapeDtypeStruct((batch * seq, width), BF16),
        scratch_shapes=[pltpu.VMEM((L + SUBLANES, width), F32),
                        pltpu.VMEM((MLSTM_HEADS, hd, hd), F32),
                        pltpu.VMEM((MLSTM_HEADS, 1, hd), F32),
                        pltpu.VMEM((MLSTM_HEADS, SUBLANES, LANES), F32)],
        compiler_params=_params(("arbitrary", "arbitrary")),
        name="l0_mlstm",
    )(uf, uz, p["cw"], p["cb"], p["wq"], p["wk"], p["wv"], p["wg"], p["bg"], p["skip"], p["gn"])


def _out_kernel(*refs, nin):
    y_refs = refs[:nin]
    w_ref, g_ref, x_ref, o_ref = refs[nin:]
    acc = jnp.dot(y_refs[0][...], w_ref[0], preferred_element_type=F32)
    for n in range(1, nin):
        acc = acc + jnp.dot(y_refs[n][...], w_ref[n], preferred_element_type=F32)
    o_ref[...] = x_ref[...] + _rms_norm(acc, g_ref[...])


def _out_proj(ys, w_bf16, g, x2, tm=256):
    nin, kin, d = w_bf16.shape
    t = x2.shape[0]
    return pl.pallas_call(
        functools.partial(_out_kernel, nin=nin),
        grid=(t // tm,),
        in_specs=[pl.BlockSpec((tm, kin), lambda i: (i, 0)) for _ in range(nin)]
        + [pl.BlockSpec((nin, kin, d), lambda i: (0, 0, 0), pipeline_mode=pl.Buffered(1)),
           pl.BlockSpec((1, d), lambda i: (0, 0)),
           pl.BlockSpec((tm, d), lambda i: (i, 0))],
        out_specs=pl.BlockSpec((tm, d), lambda i: (i, 0)),
        out_shape=jax.ShapeDtypeStruct((t, d), F32),
        compiler_params=_params(("arbitrary",)),
        name="out_proj",
    )(*ys, w_bf16, g, x2)


def _l1_in_kernel(x_ref, g_ref, w_ref, muw_ref, mua_ref, w1_ref, a1_ref, of_ref, oz_ref, lo_ref,
                  h_ref, last_ref, *, nf, tiles_per_seq):
    i = pl.program_id(0)
    j = pl.program_id(1)

    @pl.when(j == 0)
    def _():
        h = _rms_norm(x_ref[...], g_ref[...])
        h_ref[...] = h.astype(BF16)

        @pl.when(i % tiles_per_seq == 0)
        def _():
            last_ref[...] = jnp.zeros_like(last_ref)

        row = lax.broadcasted_iota(jnp.int32, h.shape, 0)
        hprev = jnp.where(row == 0, last_ref[...], pltpu.roll(h, 1, 0))
        last_ref[...] = h[h.shape[0] - 1:h.shape[0], :]
        dh = hprev - h
        xw = h + dh * muw_ref[...]
        xa = h + dh * mua_ref[...]
        lo_ref[:, 0:LORA_PAD] = jnp.tanh(_dot(xw, w1_ref[...]))
        lo_ref[:, LORA_PAD:2 * LORA_PAD] = _dot(xa, a1_ref[...])

    acc = jnp.dot(h_ref[...], w_ref[...], preferred_element_type=F32)

    @pl.when(j < nf)
    def _():
        of_ref[...] = acc

    @pl.when(j >= nf)
    def _():
        oz_ref[...] = acc.astype(BF16)


def _l1_in(x2, g, w_bf16, muw, mua, w1, a1, n_f32, seq, tm=512, tn=1024):
    t, d = x2.shape
    n = w_bf16.shape[1]
    nf = n_f32 // tn
    nz = (n - n_f32) // tn
    c2 = lambda shape: pl.BlockSpec(shape, lambda i, j: (0, 0))
    return pl.pallas_call(
        functools.partial(_l1_in_kernel, nf=nf, tiles_per_seq=seq // tm),
        grid=(t // tm, nf + nz),
        in_specs=[pl.BlockSpec((tm, d), lambda i, j: (i, 0)),
                  c2((1, d)),
                  pl.BlockSpec((d, tn), lambda i, j: (0, j)),
                  c2((1, d)), c2((1, d)), c2((d, LORA_PAD)), c2((d, LORA_PAD))],
        out_specs=[pl.BlockSpec((tm, tn), lambda i, j: (i, jnp.minimum(j, nf - 1))),
                   pl.BlockSpec((tm, tn), lambda i, j: (i, jnp.maximum(j - nf, 0))),
                   pl.BlockSpec((tm, 2 * LORA_PAD), lambda i, j: (i, 0))],
        out_shape=[jax.ShapeDtypeStruct((t, n_f32), F32),
                   jax.ShapeDtypeStruct((t, n - n_f32), BF16),
                   jax.ShapeDtypeStruct((t, 2 * LORA_PAD), F32)],
        scratch_shapes=[pltpu.VMEM((tm, d), BF16), pltpu.VMEM((1, d), F32)],
        compiler_params=_params(("arbitrary", "arbitrary")),
        name="l1_in",
    )(x2, g, w_bf16, muw, mua, w1, a1)


def _shift_rows(x, last_ref):
    row = lax.broadcasted_iota(jnp.int32, x.shape, 0)
    prev = jnp.where(row == 0, last_ref[...], pltpu.roll(x, 1, 0))
    last_ref[...] = x[x.shape[0] - 1:x.shape[0], :]
    return prev


def _rwkv_kernel(r_ref, k_ref, v_ref, z_ref, lo_ref, mur_ref, muk_ref, muv_ref, w0_ref, w2_ref,
                 a0_ref, a2_ref, kk_ref, ka_ref, rk_ref, gg_ref, gb_ref, o_ref,
                 st_s, lr_s, lk_s, lv_s, y_s):
    L = RWKV_CHUNK
    G = RWKV_HEADS_PER_STEP
    D = RWKV_HEAD_DIM
    s = pl.program_id(2)

    @pl.when(s == 0)
    def _():
        st_s[...] = jnp.zeros_like(st_s)
        lr_s[...] = jnp.zeros_like(lr_s)
        lk_s[...] = jnp.zeros_like(lk_s)
        lv_s[...] = jnp.zeros_like(lv_s)

    r = r_ref[...]
    k = k_ref[...]
    v = v_ref[...]
    r = r + (_shift_rows(r, lr_s) - r) * mur_ref[...]
    k = k + (_shift_rows(k, lk_s) - k) * muk_ref[...]
    v = v + (_shift_rows(v, lv_s) - v) * muv_ref[...]

    lo = lo_ref[...]
    w_log = -_softplus(-(w0_ref[...] + _dot(lo[:, 0:LORA_PAD], w2_ref[...]))) - 0.5
    logw = -jnp.exp(w_log)
    a = _sigmoid(a0_ref[...] + _dot(lo[:, LORA_PAD:2 * LORA_PAD], a2_ref[...]))

    width = r.shape[1]
    ri = lax.broadcasted_iota(jnp.int32, (width, width), 0) // D
    ci = lax.broadcasted_iota(jnp.int32, (width, width), 1) // D
    seg = (ri == ci).astype(BF16)

    kk = k * kk_ref[...]
    ss = _dot_exact_rhs(kk * kk, seg)
    kk = kk / jnp.maximum(jnp.sqrt(ss), 1e-12)
    k = k * (1.0 + (a - 1.0) * ka_ref[...])

    cs = _dot_exact_lhs(_tri(L).astype(BF16), logw)
    p_in = jnp.exp(cs)
    p_inv = jnp.exp(-cs)
    p_ex = jnp.exp(cs - logw)
    p_last = p_in[L - 1:L, :]

    rt = r * p_in
    at = -kk * p_ex
    kt = k * p_inv
    bt = kk * a * p_inv
    kh_all = kt * p_last
    bh_all = bt * p_last

    plc = []
    for cb in range(width // LANES):
        blk = jnp.broadcast_to(p_last[:, cb * LANES:(cb + 1) * LANES], (LANES, LANES))
        plc.append(blk.T)

    lower = _tri(L)
    strict = _tri(L, inclusive=False)
    eye = (lax.broadcasted_iota(jnp.int32, (L, L), 0)
           == lax.broadcasted_iota(jnp.int32, (L, L), 1)).astype(F32)
    per_blk = LANES // D
    bonus_w = r * k * rk_ref[...]

    for h in range(G):
        sl = slice(h * D, (h + 1) * D)
        xh = jnp.concatenate([at[:, sl], rt[:, sl]], axis=0)
        yh = jnp.concatenate([bt[:, sl], kt[:, sl]], axis=0)
        vh = v[:, sl]
        aall = _dot_nt(xh, yh)
        a_ab = jnp.where(strict, aall[0:L, 0:L], 0.0)
        a_ak = jnp.where(strict, aall[0:L, L:2 * L], 0.0)
        a_rb = jnp.where(lower, aall[L:2 * L, 0:L], 0.0)
        a_rk = jnp.where(lower, aall[L:2 * L, L:2 * L], 0.0)
        tinv = eye + a_ab
        pw = _dot(a_ab, a_ab)
        for _ in range(L.bit_length() - 3):
            both = _dot(jnp.concatenate([tinv, pw], axis=0), pw)
            tinv = tinv + both[0:L]
            pw = both[L:2 * L]
        tinv = tinv + _dot(tinv, pw)
        st = st_s[h]
        sx = _dot(xh, st)
        av = _dot(jnp.concatenate([a_ak, a_rk], axis=0), vh)
        u = _dot(tinv, sx[0:L] + av[0:L])
        yv = sx[L:2 * L] + av[L:2 * L] + _dot(a_rb, u)
        pcol = plc[h // per_blk][(h % per_blk) * D:(h % per_blk + 1) * D, 0:D]
        upd = _dot_tn(jnp.concatenate([bh_all[:, sl], kh_all[:, sl]], axis=0),
                      jnp.concatenate([u, vh], axis=0))
        st_s[h] = pcol * st + upd
        mu = jnp.mean(yv, axis=-1, keepdims=True)
        cen = yv - mu
        var = jnp.mean(cen * cen, axis=-1, keepdims=True)
        yn = cen * lax.rsqrt(var + RWKV_GN_EPS)
        bonus = jnp.sum(bonus_w[:, sl], axis=-1, keepdims=True) * vh
        y_s[:, sl] = yn * gg_ref[:, sl] + gb_ref[:, sl] + bonus
    o_ref[...] = (y_s[...] * _silu(z_ref[...].astype(F32))).astype(BF16)


def _l1_rwkv(uf, uz, lo, p, batch, seq, width):
    L = RWKV_CHUNK
    G = RWKV_HEADS_PER_STEP
    D = RWKV_HEAD_DIM
    wt = G * D
    ngrp = width // wt
    nst = seq // L
    vec = lambda: pl.BlockSpec((1, wt), lambda b, g, s: (0, g))
    col = lambda off: pl.BlockSpec((L, wt), lambda b, g, s: (b * nst + s, off + g))
    return pl.pallas_call(
        _rwkv_kernel,
        grid=(batch, ngrp, nst),
        in_specs=[col(0), col(ngrp), col(2 * ngrp), col(0),
                  pl.BlockSpec((L, 2 * LORA_PAD), lambda b, g, s: (b * nst + s, 0)),
                  vec(), vec(), vec(), vec(),
                  pl.BlockSpec((LORA_PAD, wt), lambda b, g, s: (0, g)),
                  vec(),
                  pl.BlockSpec((LORA_PAD, wt), lambda b, g, s: (0, g)),
                  vec(), vec(), vec(), vec(), vec()],
        out_specs=pl.BlockSpec((L, wt), lambda b, g, s: (b * nst + s, g)),
        out_shape=jax.ShapeDtypeStruct((batch * seq, width), BF16),
        scratch_shapes=[pltpu.VMEM((G, D, D), F32),
                        pltpu.VMEM((1, wt), F32), pltpu.VMEM((1, wt), F32),
                        pltpu.VMEM((1, wt), F32),
                        pltpu.VMEM((L, wt), F32)],
        compiler_params=_params(("arbitrary", "arbitrary", "arbitrary")),
        name="l1_rwkv",
    )(uf, uf, uf, uz, lo, p["mur"], p["muk"], p["muv"], p["w0"], p["w2"], p["a0"], p["a2"],
      p["kk"], p["ka"], p["rk"], p["gg"], p["gb"])


def _bd4_diagonals(w, width):
    nb = w.shape[0]
    rows = []
    for delta in range(-3, 4):
        cols = []
        for j in range(MLSTM_QKV_BLOCK):
            i = j - delta
            cols.append(w[:, i, j] if 0 <= i < MLSTM_QKV_BLOCK else jnp.zeros((nb,), w.dtype))
        rows.append(jnp.stack(cols, axis=1).reshape(width))
    rows.append(jnp.zeros((width,), w.dtype))
    return jnp.stack(rows, axis=0).astype(F32)


def _row(v):
    return v.reshape(1, -1).astype(F32)


def _pad_rows(w, rows):
    return jnp.pad(w, ((0, rows - w.shape[0]), (0, 0)))


def _pad_cols(w, cols):
    return jnp.pad(w, ((0, 0), (0, cols - w.shape[1])))


def kernel(x, l0_norm_pre, l0_w_in, l0_lru_conv_w, l0_lru_conv_b, l0_lru_wa, l0_lru_ba,
           l0_lru_wx, l0_lru_bx, l0_lru_lambda, l0_m_conv_w, l0_m_conv_b, l0_m_wq, l0_m_wk,
           l0_m_wv, l0_m_wi, l0_m_bi, l0_m_wf, l0_m_bf, l0_m_skip, l0_m_gn, l0_w_out,
           l0_norm_post, l1_norm_pre, l1_w_in, l1_mu_rkv, l1_mu_w, l1_mu_a, l1_w0, l1_w1,
           l1_w2, l1_a0, l1_a1, l1_a2, l1_k_k, l1_k_a, l1_r_k, l1_gn_g, l1_gn_b, l1_w_out,
           l1_norm_post):
    batch, seq, d = x.shape
    x2 = x.reshape(batch * seq, d).astype(F32)
    rw = l0_lru_lambda.shape[0]
    mw = l0_m_skip.shape[0]
    ww = l1_w0.shape[0]
    assert rw == mw, "column blocking assumes equal head-group widths"

    w_in0 = jnp.concatenate([l0_w_in[:, 0:rw], l0_w_in[:, 2 * rw:2 * rw + mw],
                             l0_w_in[:, rw:2 * rw], l0_w_in[:, 2 * rw + mw:]], axis=1).astype(BF16)
    uf0, uz0 = _l0_in(x2, _row(l0_norm_pre), w_in0, rw + mw)

    lru_p = dict(cw=l0_lru_conv_w.astype(F32), cb=_row(l0_lru_conv_b),
                 wa=l0_lru_wa.astype(BF16), ba=_row(l0_lru_ba),
                 wx=l0_lru_wx.astype(BF16), bx=_row(l0_lru_bx), lam=_row(l0_lru_lambda))
    yr = _l0_lru(uf0, uz0, lru_p, batch, seq, rw)

    wg = jnp.concatenate([l0_m_wi, l0_m_wf], axis=1)
    wg = _pad_cols(wg, LANES).reshape(3, mw, LANES).astype(BF16)
    bg = _pad_cols(jnp.concatenate([l0_m_bi, l0_m_bf]).reshape(1, -1), LANES).astype(F32)
    m_p = dict(cw=l0_m_conv_w.astype(F32), cb=_row(l0_m_conv_b),
               wq=_bd4_diagonals(l0_m_wq, mw), wk=_bd4_diagonals(l0_m_wk, mw),
               wv=_bd4_diagonals(l0_m_wv, mw), wg=wg, bg=bg,
               skip=_row(l0_m_skip), gn=_row(l0_m_gn))
    ym = _l0_mlstm(uf0, uz0, m_p, batch, seq, mw, x_col=rw // mw, z_col=rw // mw)

    w_out0 = l0_w_out.astype(BF16).reshape(2, rw, d)
    x1 = _out_proj([yr, ym], w_out0, _row(l0_norm_post), x2)

    w1 = _pad_cols(l1_w1, LORA_PAD).astype(BF16)
    a1 = _pad_cols(l1_a1, LORA_PAD).astype(BF16)
    uf1, uz1, lo1 = _l1_in(x1, _row(l1_norm_pre), l1_w_in.astype(BF16), _row(l1_mu_w),
                           _row(l1_mu_a), w1, a1, 3 * ww, seq)
    r_p = dict(mur=_row(l1_mu_rkv[0:ww]), muk=_row(l1_mu_rkv[ww:2 * ww]),
               muv=_row(l1_mu_rkv[2 * ww:]), w0=_row(l1_w0),
               w2=_pad_rows(l1_w2, LORA_PAD).astype(BF16), a0=_row(l1_a0),
               a2=_pad_rows(l1_a2, LORA_PAD).astype(BF16), kk=_row(l1_k_k), ka=_row(l1_k_a),
               rk=_row(l1_r_k), gg=_row(l1_gn_g), gb=_row(l1_gn_b))
    y1 = _l1_rwkv(uf1, uz1, lo1, r_p, batch, seq, ww)
    out = _out_proj([y1], l1_w_out.astype(BF16).reshape(1, ww, d), _row(l1_norm_post), x1)
    return out.reshape(batch, seq, d).astype(x.dtype)
```

```python
import functools

import jax
import jax.numpy as jnp
from jax import lax
from jax.experimental import pallas as pl
from jax.experimental.pallas import tpu as pltpu

F32 = jnp.float32
BF16 = jnp.bfloat16

RMS_EPS = 1e-6
CONV_WIDTH = 4
LRU_C = 8.0
LRU_BLOCK = 128
MLSTM_HEADS = 8
MLSTM_QKV_BLOCK = 4
MLSTM_CHUNK = 128
MLSTM_GN_EPS = 1e-6
RWKV_HEAD_DIM = 64
RWKV_GN_EPS = 64e-5
RWKV_CHUNK = 64
RWKV_QUAD_WIDTH = 256
RWKV_QUADS_PER_STEP = 4
LORA_PAD = 128

LANES = 128
SUBLANES = 8
VMEM_LIMIT = 56 * 1024 * 1024


def _params(sem):
    return pltpu.CompilerParams(dimension_semantics=sem, vmem_limit_bytes=VMEM_LIMIT)


def _sigmoid(x):
    return 1.0 / (1.0 + jnp.exp(-x))


def _silu(x):
    return x * _sigmoid(x)


def _softplus(x):
    return jnp.maximum(x, 0.0) + jnp.log1p(jnp.exp(-jnp.abs(x)))


def _dot(a, b):
    return jnp.dot(a.astype(BF16), b.astype(BF16), preferred_element_type=F32)


def _dot_nt(a, b):
    return lax.dot_general(a.astype(BF16), b.astype(BF16), (((1,), (1,)), ((), ())),
                           preferred_element_type=F32)


def _dot_tn(a, b):
    return lax.dot_general(a.astype(BF16), b.astype(BF16), (((0,), (0,)), ((), ())),
                           preferred_element_type=F32)


def _split3(x):
    hi = x.astype(BF16)
    r1 = x - hi.astype(F32)
    mid = r1.astype(BF16)
    lo = (r1 - mid.astype(F32)).astype(BF16)
    return hi, mid, lo


def _dot_exact_rhs(a_f32, b_bf16):
    hi, mid, lo = _split3(a_f32)
    out = jnp.dot(hi, b_bf16, preferred_element_type=F32)
    out = out + jnp.dot(mid, b_bf16, preferred_element_type=F32)
    return out + jnp.dot(lo, b_bf16, preferred_element_type=F32)


def _dot_exact_lhs(a_bf16, b_f32):
    hi, mid, lo = _split3(b_f32)
    out = jnp.dot(a_bf16, hi, preferred_element_type=F32)
    out = out + jnp.dot(a_bf16, mid, preferred_element_type=F32)
    return out + jnp.dot(a_bf16, lo, preferred_element_type=F32)


def _rms_norm(x, g):
    return x * lax.rsqrt(jnp.mean(x * x, axis=-1, keepdims=True) + RMS_EPS) * g


def _tri(n, inclusive=True):
    r = lax.broadcasted_iota(jnp.int32, (n, n), 0)
    c = lax.broadcasted_iota(jnp.int32, (n, n), 1)
    return (c <= r) if inclusive else (c < r)


def _l0_in_kernel(x_ref, g_ref, w_ref, of_ref, oz_ref, h_ref, *, nf):
    j = pl.program_id(1)

    @pl.when(j == 0)
    def _():
        h_ref[...] = _rms_norm(x_ref[...], g_ref[...]).astype(BF16)

    acc = jnp.dot(h_ref[...], w_ref[...], preferred_element_type=F32)

    @pl.when(j < nf)
    def _():
        of_ref[...] = acc

    @pl.when(j >= nf)
    def _():
        oz_ref[...] = acc.astype(BF16)


def _l0_in(x2, g, w_bf16, n_f32, tm=512, tn=1024):
    t, d = x2.shape
    n = w_bf16.shape[1]
    nf = n_f32 // tn
    nz = (n - n_f32) // tn
    return pl.pallas_call(
        functools.partial(_l0_in_kernel, nf=nf),
        grid=(t // tm, nf + nz),
        in_specs=[pl.BlockSpec((tm, d), lambda i, j: (i, 0)),
                  pl.BlockSpec((1, d), lambda i, j: (0, 0)),
                  pl.BlockSpec((d, tn), lambda i, j: (0, j))],
        out_specs=[pl.BlockSpec((tm, tn), lambda i, j: (i, jnp.minimum(j, nf - 1))),
                   pl.BlockSpec((tm, tn), lambda i, j: (i, jnp.maximum(j - nf, 0)))],
        out_shape=[jax.ShapeDtypeStruct((t, n_f32), F32),
                   jax.ShapeDtypeStruct((t, n - n_f32), BF16)],
        scratch_shapes=[pltpu.VMEM((tm, d), BF16)],
        compiler_params=_params(("arbitrary", "arbitrary")),
        name="l0_in",
    )(x2, g, w_bf16)


def _conv_from_buf(xbuf, ts, w, b):
    y = b
    for j in range(CONV_WIDTH):
        off = SUBLANES - (CONV_WIDTH - 1) + j
        y = y + xbuf[pl.ds(off, ts), :] * w[j:j + 1, :]
    return y


def _lru_kernel(x_ref, z_ref, cw_ref, cb_ref, wa_ref, ba_ref, wx_ref, bx_ref, lam_ref, o_ref,
                xbuf, a_s, u_s, hc_s, *, ts, nblk):
    s = pl.program_id(2)

    @pl.when(s == 0)
    def _():
        xbuf[pl.ds(0, SUBLANES), :] = jnp.zeros((SUBLANES, xbuf.shape[1]), F32)
        hc_s[...] = jnp.zeros_like(hc_s)

    xbuf[pl.ds(SUBLANES, ts), :] = x_ref[...]
    xc = _conv_from_buf(xbuf, ts, cw_ref[...], cb_ref[...])
    xbuf[pl.ds(0, SUBLANES), :] = xbuf[pl.ds(ts, SUBLANES), :]

    gr, gi = [], []
    for n in range(nblk):
        xn = xc[:, n * LRU_BLOCK:(n + 1) * LRU_BLOCK]
        gr.append(_dot(xn, wa_ref[n]))
        gi.append(_dot(xn, wx_ref[n]))
    r = _sigmoid(jnp.concatenate(gr, axis=1) + ba_ref[...])
    i = _sigmoid(jnp.concatenate(gi, axis=1) + bx_ref[...])
    log_a = -LRU_C * r * _softplus(-lam_ref[...])
    a = jnp.exp(log_a)
    a_s[...] = a
    u_s[...] = jnp.sqrt(-jnp.tanh(log_a) * (a * a + 1.0)) * (i * xc)

    row = lax.broadcasted_iota(jnp.int32, (SUBLANES, a_s.shape[1]), 0)

    def body(g, carry):
        r0 = pl.multiple_of(g * SUBLANES, SUBLANES)
        a = a_s[pl.ds(r0, SUBLANES), :]
        u = u_s[pl.ds(r0, SUBLANES), :]
        for sft in (1, 2, 4):
            a_sh = jnp.where(row >= sft, pltpu.roll(a, sft, 0), 1.0)
            u_sh = jnp.where(row >= sft, pltpu.roll(u, sft, 0), 0.0)
            u = a * u_sh + u
            a = a * a_sh
        h = a * carry + u
        u_s[pl.ds(r0, SUBLANES), :] = h
        return h[SUBLANES - 1:SUBLANES, :]

    hc_s[...] = lax.fori_loop(0, ts // SUBLANES, body, hc_s[...])
    o_ref[...] = (u_s[...] * _silu(z_ref[...].astype(F32))).astype(BF16)


def _l0_lru(uf, uz, p, batch, seq, width, ts=256, tc=256):
    nblk = tc // LRU_BLOCK
    nct = width // tc
    nst = seq // ts
    vec = lambda: pl.BlockSpec((1, tc), lambda b, c, s: (0, c))
    return pl.pallas_call(
        functools.partial(_lru_kernel, ts=ts, nblk=nblk),
        grid=(batch, nct, nst),
        in_specs=[pl.BlockSpec((ts, tc), lambda b, c, s: (b * nst + s, c)),
                  pl.BlockSpec((ts, tc), lambda b, c, s: (b * nst + s, c)),
                  pl.BlockSpec((CONV_WIDTH, tc), lambda b, c, s: (0, c)),
                  vec(),
                  pl.BlockSpec((nblk, LRU_BLOCK, LRU_BLOCK), lambda b, c, s: (c, 0, 0)),
                  vec(),
                  pl.BlockSpec((nblk, LRU_BLOCK, LRU_BLOCK), lambda b, c, s: (c, 0, 0)),
                  vec(), vec()],
        out_specs=pl.BlockSpec((ts, tc), lambda b, c, s: (b * nst + s, c)),
        out_shape=jax.ShapeDtypeStruct((batch * seq, width), BF16),
        scratch_shapes=[pltpu.VMEM((ts + SUBLANES, tc), F32),
                        pltpu.VMEM((ts, tc), F32),
                        pltpu.VMEM((ts, tc), F32),
                        pltpu.VMEM((1, tc), F32)],
        compiler_params=_params(("arbitrary", "arbitrary", "arbitrary")),
        name="l0_lru",
    )(uf, uz, p["cw"], p["cb"], p["wa"], p["ba"], p["wx"], p["bx"], p["lam"])


def _bd4(x, wd_ref):
    width = x.shape[1]
    outs = []
    for cb in range(width // LANES):
        sl = slice(cb * LANES, (cb + 1) * LANES)
        xb = x[:, sl]
        acc = xb * wd_ref[3:4, sl]
        for delta in (-3, -2, -1, 1, 2, 3):
            acc = acc + pltpu.roll(xb, delta % LANES, 1) * wd_ref[3 + delta:4 + delta, sl]
        outs.append(acc)
    return jnp.concatenate(outs, axis=1)


def _mlstm_kernel(x_ref, z_ref, cw_ref, cb_ref, wq_ref, wk_ref, wv_ref, wg_ref, bg_ref,
                  skip_ref, gn_ref, o_ref, xbuf, c_s, n_s, m_s, *, hd):
    L = MLSTM_CHUNK
    H = MLSTM_HEADS
    s = pl.program_id(1)

    @pl.when(s == 0)
    def _():
        xbuf[pl.ds(0, SUBLANES), :] = jnp.zeros((SUBLANES, xbuf.shape[1]), F32)
        c_s[...] = jnp.zeros_like(c_s)
        n_s[...] = jnp.zeros_like(n_s)
        m_s[...] = jnp.zeros_like(m_s)

    x = x_ref[...]
    xbuf[pl.ds(SUBLANES, L), :] = x
    xmc = _silu(_conv_from_buf(xbuf, L, cw_ref[...], cb_ref[...]))
    xbuf[pl.ds(0, SUBLANES), :] = xbuf[pl.ds(L, SUBLANES), :]

    q = _bd4(xmc, wq_ref)
    k = _bd4(xmc, wk_ref)
    v = _bd4(x, wv_ref)

    g = _dot(q, wg_ref[0]) + _dot(k, wg_ref[1]) + _dot(v, wg_ref[2]) + bg_ref[...]
    lane = lax.broadcasted_iota(jnp.int32, (L, LANES), 1)
    lf = jnp.minimum(g, 0.0) - jnp.log1p(jnp.exp(-jnp.abs(g)))
    lf = jnp.where((lane >= H) & (lane < 2 * H), lf, 0.0)
    csum = _dot_exact_lhs(_tri(L).astype(BF16), lf)
    gb = jnp.where(lane < H, g, csum)
    gbt = gb.T

    causal = _tri(L)
    kscale = hd ** -0.5
    heads = range(H)
    sls = [slice(h * hd, (h + 1) * hd) for h in heads]
    qb = [q[:, sl].astype(BF16) for sl in sls]
    kh = [k[:, sl] * kscale for sl in sls]
    vb = [v[:, sl].astype(BF16) for sl in sls]
    s_raw = [lax.dot_general(qb[h], kh[h].astype(BF16), (((1,), (1,)), ((), ())),
                             preferred_element_type=F32) for h in heads]
    qc = [jnp.dot(qb[h], c_s[h].astype(BF16), preferred_element_type=F32) for h in heads]

    wgt, inter, m_t, decay, kw, m_new = [], [], [], [], [], []
    for h in heads:
        b_col = gb[:, H + h:H + h + 1]
        i_col = gb[:, h:h + 1]
        b_row = gbt[H + h:H + h + 1, :]
        i_row = gbt[h:h + 1, :]
        m_prev = m_s[h][0:1, 0:1]
        dm = jnp.where(causal, b_col - b_row + i_row, -1e30)
        m_inter = b_col + m_prev
        mt = jnp.maximum(jnp.max(dm, axis=-1, keepdims=True), m_inter)
        wgt.append(jnp.exp(dm - mt))
        inter.append(jnp.exp(m_inter - mt))
        m_t.append(mt)
        b_last = b_col[L - 1:L, :]
        g_row = b_last - b_row + i_row
        g_col = b_last - b_col + i_col
        mn = jnp.maximum(b_last + m_prev, jnp.max(g_row, axis=-1, keepdims=True))
        m_new.append(mn)
        decay.append(jnp.exp(b_last + m_prev - mn))
        kw.append(kh[h] * jnp.exp(g_col - mn))

    upd = [lax.dot_general(kw[h].astype(BF16), vb[h], (((0,), (0,)), ((), ())),
                           preferred_element_type=F32) for h in heads]
    scores = [s_raw[h] * wgt[h] for h in heads]
    num = [jnp.dot(scores[h].astype(BF16), vb[h], preferred_element_type=F32)
           + inter[h] * qc[h] for h in heads]

    ys = []
    for h in heads:
        nvec = n_s[h]
        den = (jnp.sum(scores[h], axis=-1, keepdims=True)
               + inter[h] * jnp.sum(q[:, sls[h]] * nvec, axis=-1, keepdims=True))
        hout = num[h] / jnp.maximum(jnp.abs(den), jnp.exp(-m_t[h]))
        c_s[h] = decay[h] * c_s[h] + upd[h]
        n_s[h] = decay[h] * nvec + jnp.sum(kw[h], axis=0, keepdims=True)
        m_s[h] = jnp.broadcast_to(m_new[h], (SUBLANES, LANES))
        mu = jnp.mean(hout, axis=-1, keepdims=True)
        cen = hout - mu
        var = jnp.mean(cen * cen, axis=-1, keepdims=True)
        ys.append(cen * lax.rsqrt(var + MLSTM_GN_EPS))
    y = jnp.concatenate(ys, axis=1) * gn_ref[...] + skip_ref[...] * xmc
    o_ref[...] = (y * _silu(z_ref[...].astype(F32))).astype(BF16)


def _l0_mlstm(uf, uz, p, batch, seq, width, x_col, z_col):
    L = MLSTM_CHUNK
    nst = seq // L
    hd = width // MLSTM_HEADS
    full = lambda shape: pl.BlockSpec(shape, lambda b, s: (0,) * len(shape))
    return pl.pallas_call(
        functools.partial(_mlstm_kernel, hd=hd),
        grid=(batch, nst),
        in_specs=[pl.BlockSpec((L, width), lambda b, s: (b * nst + s, x_col)),
                  pl.BlockSpec((L, width), lambda b, s: (b * nst + s, z_col)),
                  full((CONV_WIDTH, width)), full((1, width)),
                  full((SUBLANES, width)), full((SUBLANES, width)), full((SUBLANES, width)),
                  full((3, width, LANES)), full((1, LANES)),
                  full((1, width)), full((1, width))],
        out_specs=pl.BlockSpec((L, width), lambda b, s: (b * nst + s, 0)),
        out_shape=jax.ShapeDtypeStruct((batch * seq, width), BF16),
        scratch_shapes=[pltpu.VMEM((L + SUBLANES, width), F32),
                        pltpu.VMEM((MLSTM_HEADS, hd, hd), F32),
                        pltpu.VMEM((MLSTM_HEADS, 1, hd), F32),
                        pltpu.VMEM((MLSTM_HEADS, SUBLANES, LANES), F32)],
        compiler_params=_params(("arbitrary", "arbitrary")),
        name="l0_mlstm",
    )(uf, uz, p["cw"], p["cb"], p["wq"], p["wk"], p["wv"], p["wg"], p["bg"], p["skip"], p["gn"])


def _out_kernel(*refs, nin):
    y_refs = refs[:nin]
    w_ref, g_ref, x_ref, o_ref = refs[nin:]
    acc = jnp.dot(y_refs[0][...], w_ref[0], preferred_element_type=F32)
    for n in range(1, nin):
        acc = acc + jnp.dot(y_refs[n][...], w_ref[n], preferred_element_type=F32)
    o_ref[...] = x_ref[...] + _rms_norm(acc, g_ref[...])


def _out_proj(ys, w_bf16, g, x2, tm=256):
    nin, kin, d = w_bf16.shape
    t = x2.shape[0]
    return pl.pallas_call(
        functools.partial(_out_kernel, nin=nin),
        grid=(t // tm,),
        in_specs=[pl.BlockSpec((tm, kin), lambda i: (i, 0)) for _ in range(nin)]
        + [pl.BlockSpec((nin, kin, d), lambda i: (0, 0, 0), pipeline_mode=pl.Buffered(1)),
           pl.BlockSpec((1, d), lambda i: (0, 0)),
           pl.BlockSpec((tm, d), lambda i: (i, 0))],
        out_specs=pl.BlockSpec((tm, d), lambda i: (i, 0)),
        out_shape=jax.ShapeDtypeStruct((t, d), F32),
        compiler_params=_params(("arbitrary",)),
        name="out_proj",
    )(*ys, w_bf16, g, x2)


def _l1_in_kernel(x_ref, g_ref, w_ref, muw_ref, mua_ref, w1_ref, a1_ref, of_ref, oz_ref, lo_ref,
                  h_ref, last_ref, *, nf, tiles_per_seq):
    i = pl.program_id(0)
    j = pl.program_id(1)

    @pl.when(j == 0)
    def _():
        h = _rms_norm(x_ref[...], g_ref[...])
        h_ref[...] = h.astype(BF16)

        @pl.when(i % tiles_per_seq == 0)
        def _():
            last_ref[...] = jnp.zeros_like(last_ref)

        row = lax.broadcasted_iota(jnp.int32, h.shape, 0)
        hprev = jnp.where(row == 0, last_ref[...], pltpu.roll(h, 1, 0))
        last_ref[...] = h[h.shape[0] - 1:h.shape[0], :]
        dh = hprev - h
        xw = h + dh * muw_ref[...]
        xa = h + dh * mua_ref[...]
        lo_ref[:, 0:LORA_PAD] = jnp.tanh(_dot(xw, w1_ref[...]))
        lo_ref[:, LORA_PAD:2 * LORA_PAD] = _dot(xa, a1_ref[...])

    acc = jnp.dot(h_ref[...], w_ref[...], preferred_element_type=F32)

    @pl.when(j < nf)
    def _():
        of_ref[...] = acc

    @pl.when(j >= nf)
    def _():
        oz_ref[...] = acc.astype(BF16)


def _l1_in(x2, g, w_bf16, muw, mua, w1, a1, n_f32, seq, tm=512, tn=1024):
    t, d = x2.shape
    n = w_bf16.shape[1]
    nf = n_f32 // tn
    nz = (n - n_f32) // tn
    c2 = lambda shape: pl.BlockSpec(shape, lambda i, j: (0, 0))
    return pl.pallas_call(
        functools.partial(_l1_in_kernel, nf=nf, tiles_per_seq=seq // tm),
        grid=(t // tm, nf + nz),
        in_specs=[pl.BlockSpec((tm, d), lambda i, j: (i, 0)),
                  c2((1, d)),
                  pl.BlockSpec((d, tn), lambda i, j: (0, j)),
                  c2((1, d)), c2((1, d)), c2((d, LORA_PAD)), c2((d, LORA_PAD))],
        out_specs=[pl.BlockSpec((tm, tn), lambda i, j: (i, jnp.minimum(j, nf - 1))),
                   pl.BlockSpec((tm, tn), lambda i, j: (i, jnp.maximum(j - nf, 0))),
                   pl.BlockSpec((tm, 2 * LORA_PAD), lambda i, j: (i, 0))],
        out_shape=[jax.ShapeDtypeStruct((t, n_f32), F32),
                   jax.ShapeDtypeStruct((t, n - n_f32), BF16),
                   jax.ShapeDtypeStruct((t, 2 * LORA_PAD), F32)],
        scratch_shapes=[pltpu.VMEM((tm, d), BF16), pltpu.VMEM((1, d), F32)],
        compiler_params=_params(("arbitrary", "arbitrary")),
        name="l1_in",
    )(x2, g, w_bf16, muw, mua, w1, a1)


def _shift_rows(x, last_ref):
    row = lax.broadcasted_iota(jnp.int32, x.shape, 0)
    prev = jnp.where(row == 0, last_ref[...], pltpu.roll(x, 1, 0))
    last_ref[...] = x[x.shape[0] - 1:x.shape[0], :]
    return prev


def _bdot(a, b):
    return jnp.dot(a, b, preferred_element_type=F32)


def _split2_dot(x, m_bf16):
    hi = x.astype(BF16)
    mid = (x - hi.astype(F32)).astype(BF16)
    return _bdot(hi, m_bf16) + _bdot(mid, m_bf16)


def _rwkv_kernel(r_ref, k_ref, v_ref, z_ref, lo_ref, mur_ref, muk_ref, muv_ref, w0_ref, w2_ref,
                 a0_ref, a2_ref, kk_ref, ka_ref, rk_ref, gg_ref, gb_ref, o_ref,
                 st_s, lr_s, lk_s, lv_s):
    L = RWKV_CHUNK
    D = RWKV_HEAD_DIM
    QW = RWKV_QUAD_WIDTH
    HQ = QW // D
    NQ = r_ref.shape[1] // QW
    s = pl.program_id(2)

    @pl.when(s == 0)
    def _():
        st_s[...] = jnp.zeros_like(st_s)
        lr_s[...] = jnp.zeros_like(lr_s)
        lk_s[...] = jnp.zeros_like(lk_s)
        lv_s[...] = jnp.zeros_like(lv_s)

    r = r_ref[...]
    k = k_ref[...]
    v = v_ref[...]
    r = r + (_shift_rows(r, lr_s) - r) * mur_ref[...]
    k = k + (_shift_rows(k, lk_s) - k) * muk_ref[...]
    v = v + (_shift_rows(v, lv_s) - v) * muv_ref[...]

    lo = lo_ref[...]
    w_log = -_softplus(-(w0_ref[...] + _dot(lo[:, 0:LORA_PAD], w2_ref[...]))) - 0.5
    logw = -jnp.exp(w_log)
    a = _sigmoid(a0_ref[...] + _dot(lo[:, LORA_PAD:2 * LORA_PAD], a2_ref[...]))

    seg = ((lax.broadcasted_iota(jnp.int32, (QW, QW), 0) // D)
           == (lax.broadcasted_iota(jnp.int32, (QW, QW), 1) // D))
    seg_b = seg.astype(BF16)
    qs = [slice(q * QW, (q + 1) * QW) for q in range(NQ)]

    kk = k * kk_ref[...]
    kk2 = kk * kk
    ss = jnp.concatenate([_split2_dot(kk2[:, sl], seg_b) for sl in qs], axis=1)
    kk = kk / jnp.maximum(jnp.sqrt(ss), 1e-12)
    k = k * (1.0 + (a - 1.0) * ka_ref[...])
    bw = r * k * rk_ref[...]
    bonus = jnp.concatenate([_split2_dot(bw[:, sl], seg_b) for sl in qs], axis=1) * v

    cs = _dot_exact_lhs(_tri(L).astype(BF16), logw)
    p_in = jnp.exp(cs)
    p_inv = jnp.exp(-cs)
    p_ex = jnp.exp(cs - logw)
    p_last = p_in[L - 1:L, :]

    rt = r * p_in
    at = -kk * p_ex
    kt = k * p_inv
    bt = kk * a * p_inv
    kh = kt * p_last
    bh = bt * p_last

    lane = lax.broadcasted_iota(jnp.int32, (1, QW), 1) // D
    hmb = [(lane == h).astype(BF16) for h in range(HQ)]
    rowi = lax.broadcasted_iota(jnp.int32, (L, QW), 0)
    coli = lax.broadcasted_iota(jnp.int32, (L, QW), 1) % D
    strict = coli < rowi
    lower = coli <= rowi
    eye = (coli == rowi).astype(F32)

    def bd(m):
        mb = m.astype(BF16)
        return jnp.concatenate([mb * hm for hm in hmb], axis=0)

    def cat2(x, y):
        return jnp.concatenate([x, y], axis=0)

    xs, vs, a_ab, a_ak, a_rb, a_rk = [], [], [], [], [], []
    for sl in qs:
        xq = cat2(at[:, sl], rt[:, sl]).astype(BF16)
        btb = bt[:, sl].astype(BF16)
        ktb = kt[:, sl].astype(BF16)
        ycat = jnp.concatenate([btb * hm for hm in hmb] + [ktb * hm for hm in hmb], axis=0)
        aall = lax.dot_general(xq, ycat, (((1,), (1,)), ((), ())),
                               preferred_element_type=F32)
        xs.append(xq)
        vs.append(v[:, sl])
        a_ab.append(jnp.where(strict, aall[0:L, 0:QW], 0.0))
        a_ak.append(jnp.where(strict, aall[0:L, QW:2 * QW], 0.0))
        a_rb.append(jnp.where(lower, aall[L:2 * L, 0:QW], 0.0))
        a_rk.append(jnp.where(lower, aall[L:2 * L, QW:2 * QW], 0.0))

    tinv = [eye + m for m in a_ab]
    pw = [_bdot(m.astype(BF16), bd(m)) for m in a_ab]
    sx = [_bdot(xs[q], st_s[q].astype(BF16)) for q in range(NQ)]
    av = [_bdot(cat2(a_ak[q], a_rk[q]).astype(BF16), bd(vs[q])) for q in range(NQ)]
    for _ in range(L.bit_length() - 3):
        both = [_bdot(cat2(tinv[q], pw[q]).astype(BF16), bd(pw[q])) for q in range(NQ)]
        tinv = [tinv[q] + both[q][0:L] for q in range(NQ)]
        pw = [both[q][L:2 * L] for q in range(NQ)]
    tinv = [tinv[q] + _bdot(tinv[q].astype(BF16), bd(pw[q])) for q in range(NQ)]

    us = [_bdot(tinv[q].astype(BF16), bd(sx[q][0:L] + av[q][0:L])) for q in range(NQ)]
    ys = [sx[q][L:2 * L] + av[q][L:2 * L] + _bdot(a_rb[q].astype(BF16), bd(us[q]))
          for q in range(NQ)]

    for q, sl in enumerate(qs):
        upd = lax.dot_general(cat2(bh[:, sl], kh[:, sl]).astype(BF16),
                              cat2(us[q], vs[q]).astype(BF16), (((0,), (0,)), ((), ())),
                              preferred_element_type=F32)
        cols = []
        for cb in range(QW // LANES):
            lo_l = q * QW + cb * LANES
            cols.append(jnp.broadcast_to(p_last[:, lo_l:lo_l + LANES], (LANES, LANES)).T)
        pcol = jnp.concatenate(cols, axis=0)
        pcol = jnp.concatenate([pcol] * (QW // LANES), axis=1)
        st_s[q] = pcol * st_s[q] + jnp.where(seg, upd, 0.0)

    outs = []
    for q, sl in enumerate(qs):
        mu = _split2_dot(ys[q], seg_b) * (1.0 / D)
        cen = ys[q] - mu
        var = _split2_dot(cen * cen, seg_b) * (1.0 / D)
        outs.append(cen * lax.rsqrt(var + RWKV_GN_EPS))
    y = jnp.concatenate(outs, axis=1) * gg_ref[...] + gb_ref[...] + bonus
    o_ref[...] = (y * _silu(z_ref[...].astype(F32))).astype(BF16)


def _l1_rwkv(uf, uz, lo, p, batch, seq, width):
    L = RWKV_CHUNK
    QW = RWKV_QUAD_WIDTH
    wt = RWKV_QUADS_PER_STEP * QW
    ngrp = width // wt
    nst = seq // L
    vec = lambda: pl.BlockSpec((1, wt), lambda b, g, s: (0, g))
    col = lambda off: pl.BlockSpec((L, wt), lambda b, g, s: (b * nst + s, off + g))
    return pl.pallas_call(
        _rwkv_kernel,
        grid=(batch, ngrp, nst),
        in_specs=[col(0), col(ngrp), col(2 * ngrp), col(0),
                  pl.BlockSpec((L, 2 * LORA_PAD), lambda b, g, s: (b * nst + s, 0)),
                  vec(), vec(), vec(), vec(),
                  pl.BlockSpec((LORA_PAD, wt), lambda b, g, s: (0, g)),
                  vec(),
                  pl.BlockSpec((LORA_PAD, wt), lambda b, g, s: (0, g)),
                  vec(), vec(), vec(), vec(), vec()],
        out_specs=pl.BlockSpec((L, wt), lambda b, g, s: (b * nst + s, g)),
        out_shape=jax.ShapeDtypeStruct((batch * seq, width), BF16),
        scratch_shapes=[pltpu.VMEM((RWKV_QUADS_PER_STEP, QW, QW), F32),
                        pltpu.VMEM((1, wt), F32), pltpu.VMEM((1, wt), F32),
                        pltpu.VMEM((1, wt), F32)],
        compiler_params=_params(("arbitrary", "arbitrary", "arbitrary")),
        name="l1_rwkv",
    )(uf, uf, uf, uz, lo, p["mur"], p["muk"], p["muv"], p["w0"], p["w2"], p["a0"], p["a2"],
      p["kk"], p["ka"], p["rk"], p["gg"], p["gb"])


def _bd4_diagonals(w, width):
    nb = w.shape[0]
    rows = []
    for delta in range(-3, 4):
        cols = []
        for j in range(MLSTM_QKV_BLOCK):
            i = j - delta
            cols.append(w[:, i, j] if 0 <= i < MLSTM_QKV_BLOCK else jnp.zeros((nb,), w.dtype))
        rows.append(jnp.stack(cols, axis=1).reshape(width))
    rows.append(jnp.zeros((width,), w.dtype))
    return jnp.stack(rows, axis=0).astype(F32)


def _row(v):
    return v.reshape(1, -1).astype(F32)


def _pad_rows(w, rows):
    return jnp.pad(w, ((0, rows - w.shape[0]), (0, 0)))


def _pad_cols(w, cols):
    return jnp.pad(w, ((0, 0), (0, cols - w.shape[1])))


def kernel(x, l0_norm_pre, l0_w_in, l0_lru_conv_w, l0_lru_conv_b, l0_lru_wa, l0_lru_ba,
           l0_lru_wx, l0_lru_bx, l0_lru_lambda, l0_m_conv_w, l0_m_conv_b, l0_m_wq, l0_m_wk,
           l0_m_wv, l0_m_wi, l0_m_bi, l0_m_wf, l0_m_bf, l0_m_skip, l0_m_gn, l0_w_out,
           l0_norm_post, l1_norm_pre, l1_w_in, l1_mu_rkv, l1_mu_w, l1_mu_a, l1_w0, l1_w1,
           l1_w2, l1_a0, l1_a1, l1_a2, l1_k_k, l1_k_a, l1_r_k, l1_gn_g, l1_gn_b, l1_w_out,
           l1_norm_post):
    batch, seq, d = x.shape
    x2 = x.reshape(batch * seq, d).astype(F32)
    rw = l0_lru_lambda.shape[0]
    mw = l0_m_skip.shape[0]
    ww = l1_w0.shape[0]
    assert rw == mw, "column blocking assumes equal head-group widths"

    w_in0 = jnp.concatenate([l0_w_in[:, 0:rw], l0_w_in[:, 2 * rw:2 * rw + mw],
                             l0_w_in[:, rw:2 * rw], l0_w_in[:, 2 * rw + mw:]], axis=1).astype(BF16)
    uf0, uz0 = _l0_in(x2, _row(l0_norm_pre), w_in0, rw + mw)

    lru_p = dict(cw=l0_lru_conv_w.astype(F32), cb=_row(l0_lru_conv_b),
                 wa=l0_lru_wa.astype(BF16), ba=_row(l0_lru_ba),
                 wx=l0_lru_wx.astype(BF16), bx=_row(l0_lru_bx), lam=_row(l0_lru_lambda))
    yr = _l0_lru(uf0, uz0, lru_p, batch, seq, rw)

    wg = jnp.concatenate([l0_m_wi, l0_m_wf], axis=1)
    wg = _pad_cols(wg, LANES).reshape(3, mw, LANES).astype(BF16)
    bg = _pad_cols(jnp.concatenate([l0_m_bi, l0_m_bf]).reshape(1, -1), LANES).astype(F32)
    m_p = dict(cw=l0_m_conv_w.astype(F32), cb=_row(l0_m_conv_b),
               wq=_bd4_diagonals(l0_m_wq, mw), wk=_bd4_diagonals(l0_m_wk, mw),
               wv=_bd4_diagonals(l0_m_wv, mw), wg=wg, bg=bg,
               skip=_row(l0_m_skip), gn=_row(l0_m_gn))
    ym = _l0_mlstm(uf0, uz0, m_p, batch, seq, mw, x_col=rw // mw, z_col=rw // mw)

    w_out0 = l0_w_out.astype(BF16).reshape(2, rw, d)
    x1 = _out_proj([yr, ym], w_out0, _row(l0_norm_post), x2)

    w1 = _pad_cols(l1_w1, LORA_PAD).astype(BF16)
    a1 = _pad_cols(l1_a1, LORA_PAD).astype(BF16)
    uf1, uz1, lo1 = _l1_in(x1, _row(l1_norm_pre), l1_w_in.astype(BF16), _row(l1_mu_w),
                           _row(l1_mu_a), w1, a1, 3 * ww, seq)
    r_p = dict(mur=_row(l1_mu_rkv[0:ww]), muk=_row(l1_mu_rkv[ww:2 * ww]),
               muv=_row(l1_mu_rkv[2 * ww:]), w0=_row(l1_w0),
               w2=_pad_rows(l1_w2, LORA_PAD).astype(BF16), a0=_row(l1_a0),
               a2=_pad_rows(l1_a2, LORA_PAD).astype(BF16), kk=_row(l1_k_k), ka=_row(l1_k_a),
               rk=_row(l1_r_k), gg=_row(l1_gn_g), gb=_row(l1_gn_b))
    y1 = _l1_rwkv(uf1, uz1, lo1, r_p, batch, seq, ww)
    out = _out_proj([y1], l1_w_out.astype(BF16).reshape(1, ww, d), _row(l1_norm_post), x1)
    return out.reshape(batch, seq, d).astype(x.dtype)
```

```python
import functools

import jax
import jax.numpy as jnp
from jax import lax
from jax.experimental import pallas as pl
from jax.experimental.pallas import tpu as pltpu

F32 = jnp.float32
BF16 = jnp.bfloat16

RMS_EPS = 1e-6
CONV_WIDTH = 4
LRU_C = 8.0
LRU_BLOCK = 128
MLSTM_HEADS = 8
MLSTM_QKV_BLOCK = 4
MLSTM_CHUNK = 128
MLSTM_GN_EPS = 1e-6
RWKV_HEAD_DIM = 64
RWKV_GN_EPS = 64e-5
RWKV_CHUNK = 64
RWKV_DECAY_SCALE = 0.6065306597126334
RWKV_QUAD_WIDTH = 256
RWKV_QUADS_PER_STEP = 8
LORA_PAD = 128

LANES = 128
SUBLANES = 8
MXU_DIM = 256
VMEM_LIMIT = 56 * 1024 * 1024


def _params(sem):
    return pltpu.CompilerParams(dimension_semantics=sem, vmem_limit_bytes=VMEM_LIMIT)


def _sigmoid(x):
    return 1.0 / (1.0 + jnp.exp(-x))


def _silu(x):
    return x * _sigmoid(x)


def _softplus(x):
    return jnp.maximum(x, 0.0) + jnp.log1p(jnp.exp(-jnp.abs(x)))


def _dot(a, b):
    return jnp.dot(a.astype(BF16), b.astype(BF16), preferred_element_type=F32)


def _dot_nt(a, b):
    return lax.dot_general(a.astype(BF16), b.astype(BF16), (((1,), (1,)), ((), ())),
                           preferred_element_type=F32)


def _dot_tn(a, b):
    return lax.dot_general(a.astype(BF16), b.astype(BF16), (((0,), (0,)), ((), ())),
                           preferred_element_type=F32)


def _split3(x):
    hi = x.astype(BF16)
    r1 = x - hi.astype(F32)
    mid = r1.astype(BF16)
    lo = (r1 - mid.astype(F32)).astype(BF16)
    return hi, mid, lo


def _dot_exact_rhs(a_f32, b_bf16):
    hi, mid, lo = _split3(a_f32)
    out = jnp.dot(hi, b_bf16, preferred_element_type=F32)
    out = out + jnp.dot(mid, b_bf16, preferred_element_type=F32)
    return out + jnp.dot(lo, b_bf16, preferred_element_type=F32)


def _dot_exact_lhs(a_bf16, b_f32):
    hi, mid, lo = _split3(b_f32)
    out = jnp.dot(a_bf16, hi, preferred_element_type=F32)
    out = out + jnp.dot(a_bf16, mid, preferred_element_type=F32)
    return out + jnp.dot(a_bf16, lo, preferred_element_type=F32)


def _rms_norm(x, g):
    return x * lax.rsqrt(jnp.mean(x * x, axis=-1, keepdims=True) + RMS_EPS) * g


def _tri(n, inclusive=True):
    r = lax.broadcasted_iota(jnp.int32, (n, n), 0)
    c = lax.broadcasted_iota(jnp.int32, (n, n), 1)
    return (c <= r) if inclusive else (c < r)


def _l0_in_kernel(x_ref, g_ref, w_ref, of_ref, oz_ref, h_ref, *, nf):
    j = pl.program_id(1)

    @pl.when(j == 0)
    def _():
        h_ref[...] = _rms_norm(x_ref[...], g_ref[...]).astype(BF16)

    acc = jnp.dot(h_ref[...], w_ref[...], preferred_element_type=F32)

    @pl.when(j < nf)
    def _():
        of_ref[...] = acc

    @pl.when(j >= nf)
    def _():
        oz_ref[...] = acc.astype(BF16)


def _l0_in(x2, g, w_bf16, n_f32, tm=1024, tn=1024):
    t, d = x2.shape
    n = w_bf16.shape[1]
    nf = n_f32 // tn
    nz = (n - n_f32) // tn
    return pl.pallas_call(
        functools.partial(_l0_in_kernel, nf=nf),
        grid=(t // tm, nf + nz),
        in_specs=[pl.BlockSpec((tm, d), lambda i, j: (i, 0)),
                  pl.BlockSpec((1, d), lambda i, j: (0, 0)),
                  pl.BlockSpec((d, tn), lambda i, j: (0, j))],
        out_specs=[pl.BlockSpec((tm, tn), lambda i, j: (i, jnp.minimum(j, nf - 1))),
                   pl.BlockSpec((tm, tn), lambda i, j: (i, jnp.maximum(j - nf, 0)))],
        out_shape=[jax.ShapeDtypeStruct((t, n_f32), F32),
                   jax.ShapeDtypeStruct((t, n - n_f32), BF16)],
        scratch_shapes=[pltpu.VMEM((tm, d), BF16)],
        compiler_params=_params(("arbitrary", "arbitrary")),
        name="l0_in",
    )(x2, g, w_bf16)


def _conv_from_buf(xbuf, ts, w, b):
    y = b
    for j in range(CONV_WIDTH):
        off = SUBLANES - (CONV_WIDTH - 1) + j
        y = y + xbuf[pl.ds(off, ts), :] * w[j:j + 1, :]
    return y


def _lru_kernel(x_ref, z_ref, cw_ref, cb_ref, wa_ref, ba_ref, wx_ref, bx_ref, lam_ref, o_ref,
                xbuf, a_s, u_s, hc_s, *, ts, nblk):
    s = pl.program_id(2)

    @pl.when(s == 0)
    def _():
        xbuf[pl.ds(0, SUBLANES), :] = jnp.zeros((SUBLANES, xbuf.shape[1]), F32)
        hc_s[...] = jnp.zeros_like(hc_s)

    xbuf[pl.ds(SUBLANES, ts), :] = x_ref[...]
    xc = _conv_from_buf(xbuf, ts, cw_ref[...], cb_ref[...])
    xbuf[pl.ds(0, SUBLANES), :] = xbuf[pl.ds(ts, SUBLANES), :]

    gr, gi = [], []
    for n in range(nblk):
        xn = xc[:, n * LRU_BLOCK:(n + 1) * LRU_BLOCK]
        gr.append(_dot(xn, wa_ref[n]))
        gi.append(_dot(xn, wx_ref[n]))
    r = _sigmoid(jnp.concatenate(gr, axis=1) + ba_ref[...])
    i = _sigmoid(jnp.concatenate(gi, axis=1) + bx_ref[...])
    log_a = -LRU_C * r * _softplus(-lam_ref[...])
    a = jnp.exp(log_a)
    a_s[...] = a
    u_s[...] = jnp.sqrt(-jnp.tanh(log_a) * (a * a + 1.0)) * (i * xc)

    row = lax.broadcasted_iota(jnp.int32, (SUBLANES, a_s.shape[1]), 0)

    def body(g, carry):
        r0 = pl.multiple_of(g * SUBLANES, SUBLANES)
        a = a_s[pl.ds(r0, SUBLANES), :]
        u = u_s[pl.ds(r0, SUBLANES), :]
        for sft in (1, 2, 4):
            a_sh = jnp.where(row >= sft, pltpu.roll(a, sft, 0), 1.0)
            u_sh = jnp.where(row >= sft, pltpu.roll(u, sft, 0), 0.0)
            u = a * u_sh + u
            a = a * a_sh
        h = a * carry + u
        u_s[pl.ds(r0, SUBLANES), :] = h
        return h[SUBLANES - 1:SUBLANES, :]

    hc_s[...] = lax.fori_loop(0, ts // SUBLANES, body, hc_s[...])
    o_ref[...] = (u_s[...] * _silu(z_ref[...].astype(F32))).astype(BF16)


def _l0_lru(uf, uz, p, batch, seq, width, ts=256, tc=256):
    nblk = tc // LRU_BLOCK
    nct = width // tc
    nst = seq // ts
    vec = lambda: pl.BlockSpec((1, tc), lambda b, c, s: (0, c))
    return pl.pallas_call(
        functools.partial(_lru_kernel, ts=ts, nblk=nblk),
        grid=(batch, nct, nst),
        in_specs=[pl.BlockSpec((ts, tc), lambda b, c, s: (b * nst + s, c)),
                  pl.BlockSpec((ts, tc), lambda b, c, s: (b * nst + s, c)),
                  pl.BlockSpec((CONV_WIDTH, tc), lambda b, c, s: (0, c)),
                  vec(),
                  pl.BlockSpec((nblk, LRU_BLOCK, LRU_BLOCK), lambda b, c, s: (c, 0, 0)),
                  vec(),
                  pl.BlockSpec((nblk, LRU_BLOCK, LRU_BLOCK), lambda b, c, s: (c, 0, 0)),
                  vec(), vec()],
        out_specs=pl.BlockSpec((ts, tc), lambda b, c, s: (b * nst + s, c)),
        out_shape=jax.ShapeDtypeStruct((batch * seq, width), BF16),
        scratch_shapes=[pltpu.VMEM((ts + SUBLANES, tc), F32),
                        pltpu.VMEM((ts, tc), F32),
                        pltpu.VMEM((ts, tc), F32),
                        pltpu.VMEM((1, tc), F32)],
        compiler_params=_params(("arbitrary", "arbitrary", "arbitrary")),
        name="l0_lru",
    )(uf, uz, p["cw"], p["cb"], p["wa"], p["ba"], p["wx"], p["bx"], p["lam"])


def _bd4(xb, w_ref):
    nb, bw, _ = w_ref.shape
    return jnp.concatenate(
        [jnp.dot(xb[:, c * bw:(c + 1) * bw], w_ref[c], preferred_element_type=F32)
         for c in range(nb)], axis=1)


def _mlstm_kernel(x_ref, z_ref, cw_ref, cb_ref, wq_ref, wk_ref, wv_ref, wg_ref, bg_ref,
                  skip_ref, gn_ref, o_ref, xbuf, c_s, n_s, m_s, *, hd):
    L = MLSTM_CHUNK
    H = MLSTM_HEADS
    s = pl.program_id(1)

    @pl.when(s == 0)
    def _():
        xbuf[pl.ds(0, SUBLANES), :] = jnp.zeros((SUBLANES, xbuf.shape[1]), F32)
        c_s[...] = jnp.zeros_like(c_s)
        n_s[...] = jnp.zeros_like(n_s)
        m_s[...] = jnp.zeros_like(m_s)

    x = x_ref[...]
    xbuf[pl.ds(SUBLANES, L), :] = x
    xmc = _silu(_conv_from_buf(xbuf, L, cw_ref[...], cb_ref[...]))
    xbuf[pl.ds(0, SUBLANES), :] = xbuf[pl.ds(L, SUBLANES), :]

    xmc_b = xmc.astype(BF16)
    q = _bd4(xmc_b, wq_ref)
    k = _bd4(xmc_b, wk_ref)
    v = _bd4(x.astype(BF16), wv_ref)

    g = _dot(q, wg_ref[0]) + _dot(k, wg_ref[1]) + _dot(v, wg_ref[2]) + bg_ref[...]
    lane = lax.broadcasted_iota(jnp.int32, (L, LANES), 1)
    lf = jnp.minimum(g, 0.0) - jnp.log1p(jnp.exp(-jnp.abs(g)))
    lf = jnp.where((lane >= H) & (lane < 2 * H), lf, 0.0)
    csum = _dot_exact_lhs(_tri(L).astype(BF16), lf)
    gb = jnp.where(lane < H, g, csum)
    gbt = gb.T

    causal = _tri(L)
    kscale = hd ** -0.5
    heads = range(H)
    sls = [slice(h * hd, (h + 1) * hd) for h in heads]
    qb = [q[:, sl].astype(BF16) for sl in sls]
    kh = [k[:, sl] * kscale for sl in sls]
    vb = [v[:, sl].astype(BF16) for sl in sls]
    s_raw = [lax.dot_general(qb[h], kh[h].astype(BF16), (((1,), (1,)), ((), ())),
                             preferred_element_type=F32) for h in heads]
    qc = [jnp.dot(qb[h], c_s[h].astype(BF16), preferred_element_type=F32) for h in heads]

    wgt, inter, m_t, decay, kw, m_new = [], [], [], [], [], []
    for h in heads:
        b_col = gb[:, H + h:H + h + 1]
        i_col = gb[:, h:h + 1]
        b_row = gbt[H + h:H + h + 1, :]
        i_row = gbt[h:h + 1, :]
        m_prev = m_s[h][0:1, 0:1]
        dm = jnp.where(causal, b_col - b_row + i_row, -1e30)
        m_inter = b_col + m_prev
        mt = jnp.maximum(jnp.max(dm, axis=-1, keepdims=True), m_inter)
        wgt.append(jnp.exp(dm - mt))
        inter.append(jnp.exp(m_inter - mt))
        m_t.append(mt)
        b_last = b_col[L - 1:L, :]
        g_row = b_last - b_row + i_row
        g_col = b_last - b_col + i_col
        mn = jnp.maximum(b_last + m_prev, jnp.max(g_row, axis=-1, keepdims=True))
        m_new.append(mn)
        decay.append(jnp.exp(b_last + m_prev - mn))
        kw.append(kh[h] * jnp.exp(g_col - mn))

    upd = [lax.dot_general(kw[h].astype(BF16), vb[h], (((0,), (0,)), ((), ())),
                           preferred_element_type=F32) for h in heads]
    scores = [s_raw[h] * wgt[h] for h in heads]
    num = [jnp.dot(scores[h].astype(BF16), vb[h], preferred_element_type=F32)
           + inter[h] * qc[h] for h in heads]

    ys = []
    for h in heads:
        nvec = n_s[h]
        den = (jnp.sum(scores[h], axis=-1, keepdims=True)
               + inter[h] * jnp.sum(q[:, sls[h]] * nvec, axis=-1, keepdims=True))
        hout = num[h] / jnp.maximum(jnp.abs(den), jnp.exp(-m_t[h]))
        c_s[h] = decay[h] * c_s[h] + upd[h]
        n_s[h] = decay[h] * nvec + jnp.sum(kw[h], axis=0, keepdims=True)
        m_s[h] = jnp.broadcast_to(m_new[h], (SUBLANES, LANES))
        mu = jnp.mean(hout, axis=-1, keepdims=True)
        cen = hout - mu
        var = jnp.mean(cen * cen, axis=-1, keepdims=True)
        ys.append(cen * lax.rsqrt(var + MLSTM_GN_EPS))
    y = jnp.concatenate(ys, axis=1) * gn_ref[...] + skip_ref[...] * xmc
    o_ref[...] = (y * _silu(z_ref[...].astype(F32))).astype(BF16)


def _l0_mlstm(uf, uz, p, batch, seq, width, x_col, z_col):
    L = MLSTM_CHUNK
    nst = seq // L
    hd = width // MLSTM_HEADS
    full = lambda shape: pl.BlockSpec(shape, lambda b, s: (0,) * len(shape))
    return pl.pallas_call(
        functools.partial(_mlstm_kernel, hd=hd),
        grid=(batch, nst),
        in_specs=[pl.BlockSpec((L, width), lambda b, s: (b * nst + s, x_col)),
                  pl.BlockSpec((L, width), lambda b, s: (b * nst + s, z_col)),
                  full((CONV_WIDTH, width)), full((1, width)),
                  full((width // MXU_DIM, MXU_DIM, MXU_DIM)),
                  full((width // MXU_DIM, MXU_DIM, MXU_DIM)),
                  full((width // MXU_DIM, MXU_DIM, MXU_DIM)),
                  full((3, width, LANES)), full((1, LANES)),
                  full((1, width)), full((1, width))],
        out_specs=pl.BlockSpec((L, width), lambda b, s: (b * nst + s, 0)),
        out_shape=jax.ShapeDtypeStruct((batch * seq, width), BF16),
        scratch_shapes=[pltpu.VMEM((L + SUBLANES, width), F32),
                        pltpu.VMEM((MLSTM_HEADS, hd, hd), F32),
                        pltpu.VMEM((MLSTM_HEADS, 1, hd), F32),
                        pltpu.VMEM((MLSTM_HEADS, SUBLANES, LANES), F32)],
        compiler_params=_params(("arbitrary", "arbitrary")),
        name="l0_mlstm",
    )(uf, uz, p["cw"], p["cb"], p["wq"], p["wk"], p["wv"], p["wg"], p["bg"], p["skip"], p["gn"])


def _out_kernel(*refs, nin):
    y_refs = refs[:nin]
    w_ref, g_ref, x_ref, o_ref = refs[nin:]
    acc = jnp.dot(y_refs[0][...], w_ref[0], preferred_element_type=F32)
    for n in range(1, nin):
        acc = acc + jnp.dot(y_refs[n][...], w_ref[n], preferred_element_type=F32)
    o_ref[...] = x_ref[...] + _rms_norm(acc, g_ref[...])


def _out_proj(ys, w_bf16, g, x2, tm=256):
    nin, kin, d = w_bf16.shape
    t = x2.shape[0]
    return pl.pallas_call(
        functools.partial(_out_kernel, nin=nin),
        grid=(t // tm,),
        in_specs=[pl.BlockSpec((tm, kin), lambda i: (i, 0)) for _ in range(nin)]
        + [pl.BlockSpec((nin, kin, d), lambda i: (0, 0, 0), pipeline_mode=pl.Buffered(1)),
           pl.BlockSpec((1, d), lambda i: (0, 0)),
           pl.BlockSpec((tm, d), lambda i: (i, 0))],
        out_specs=pl.BlockSpec((tm, d), lambda i: (i, 0)),
        out_shape=jax.ShapeDtypeStruct((t, d), F32),
        compiler_params=_params(("arbitrary",)),
        name="out_proj",
    )(*ys, w_bf16, g, x2)


def _shift_rows(x, last_ref, first):
    row = lax.broadcasted_iota(jnp.int32, x.shape, 0)
    last = jnp.where(first, 0.0, last_ref[...])
    prev = jnp.where(row == 0, last, pltpu.roll(x, 1, 0))
    last_ref[...] = x[x.shape[0] - 1:x.shape[0], :]
    return prev


def _l1_in_kernel(x_ref, g_ref, w_ref, muw_ref, mua_ref, w1_ref, a1_ref, mu_ref, of_ref, oz_ref,
                  lo_ref, h_ref, last_ref, lastu_ref, *, nf, tiles_per_seq):
    i = pl.program_id(0)
    j = pl.program_id(1)
    first = i % tiles_per_seq == 0

    @pl.when(j == 0)
    def _():
        h = _rms_norm(x_ref[...], g_ref[...])
        h_ref[...] = h.astype(BF16)
        dh = _shift_rows(h, last_ref, first) - h
        xw = h + dh * muw_ref[...]
        xa = h + dh * mua_ref[...]
        lo_ref[:, 0:LORA_PAD] = jnp.tanh(_dot(xw, w1_ref[...]))
        lo_ref[:, LORA_PAD:2 * LORA_PAD] = _dot(xa, a1_ref[...])

    acc = jnp.dot(h_ref[...], w_ref[...], preferred_element_type=F32)
    mixed = acc + (_shift_rows(acc, lastu_ref.at[j], first) - acc) * mu_ref[...]

    @pl.when(j < nf)
    def _():
        of_ref[...] = mixed

    @pl.when(j >= nf)
    def _():
        oz_ref[...] = mixed.astype(BF16)


def _l1_in(x2, g, w_bf16, muw, mua, w1, a1, mu_rkv, seq, tm=1024, tn=1024):
    t, d = x2.shape
    n = w_bf16.shape[1]
    n_f32 = mu_rkv.shape[1]
    nf = n_f32 // tn
    nz = (n - n_f32) // tn
    mu_rkv = _pad_cols(mu_rkv, n)
    c2 = lambda shape: pl.BlockSpec(shape, lambda i, j: (0, 0))
    return pl.pallas_call(
        functools.partial(_l1_in_kernel, nf=nf, tiles_per_seq=seq // tm),
        grid=(t // tm, nf + nz),
        in_specs=[pl.BlockSpec((tm, d), lambda i, j: (i, 0)),
                  c2((1, d)),
                  pl.BlockSpec((d, tn), lambda i, j: (0, j)),
                  c2((1, d)), c2((1, d)), c2((d, LORA_PAD)), c2((d, LORA_PAD)),
                  pl.BlockSpec((1, tn), lambda i, j: (0, j))],
        out_specs=[pl.BlockSpec((tm, tn), lambda i, j: (i, jnp.minimum(j, nf - 1))),
                   pl.BlockSpec((tm, tn), lambda i, j: (i, jnp.maximum(j - nf, 0))),
                   pl.BlockSpec((tm, 2 * LORA_PAD), lambda i, j: (i, 0))],
        out_shape=[jax.ShapeDtypeStruct((t, n_f32), F32),
                   jax.ShapeDtypeStruct((t, n - n_f32), BF16),
                   jax.ShapeDtypeStruct((t, 2 * LORA_PAD), F32)],
        scratch_shapes=[pltpu.VMEM((tm, d), BF16), pltpu.VMEM((1, d), F32),
                        pltpu.VMEM((nf + nz, 1, tn), F32)],
        compiler_params=_params(("arbitrary", "arbitrary")),
        name="l1_in",
    )(x2, g, w_bf16, muw, mua, w1, a1, mu_rkv)


def _bdot(a, b):
    return jnp.dot(a, b, preferred_element_type=F32)


def _rwkv_kernel(r_ref, k_ref, v_ref, z_ref, lo_ref, w0_ref, w2_ref,
                 a0_ref, a2_ref, kk_ref, ka_ref, rk_ref, gg_ref, gb_ref, o_ref, st_s):
    L = RWKV_CHUNK
    D = RWKV_HEAD_DIM
    QW = RWKV_QUAD_WIDTH
    HQ = QW // D
    NQ = r_ref.shape[1] // QW
    s = pl.program_id(2)

    @pl.when(s == 0)
    def _():
        st_s[...] = jnp.zeros_like(st_s)

    r = r_ref[...]
    k = k_ref[...]
    v = v_ref[...]

    lo = lo_ref[...]
    logw = -RWKV_DECAY_SCALE * _sigmoid(w0_ref[...] + _dot(lo[:, 0:LORA_PAD], w2_ref[...]))
    a = _sigmoid(a0_ref[...] + _dot(lo[:, LORA_PAD:2 * LORA_PAD], a2_ref[...]))

    seg = ((lax.broadcasted_iota(jnp.int32, (QW, QW), 0) // D)
           == (lax.broadcasted_iota(jnp.int32, (QW, QW), 1) // D))
    seg_b = seg.astype(BF16)
    qs = [slice(q * QW, (q + 1) * QW) for q in range(NQ)]

    def segsum(x):
        return jnp.concatenate([_bdot(x[:, sl].astype(BF16), seg_b) for sl in qs], axis=1)

    kk = k * kk_ref[...]
    kk = kk * lax.rsqrt(jnp.maximum(segsum(kk * kk), 1e-24))
    k = k * (1.0 + (a - 1.0) * ka_ref[...])
    bonus = segsum(r * k * rk_ref[...]) * v

    tri_b = _tri(L).astype(BF16)
    lw_hi = logw.astype(BF16)
    lw_lo = (logw - lw_hi.astype(F32)).astype(BF16)
    cs = _bdot(tri_b, lw_hi) + _bdot(tri_b, lw_lo)
    p_in = jnp.exp(cs)
    p_inv = jnp.exp(-cs)
    p_ex = jnp.exp(cs - logw)
    p_last = p_in[L - 1:L, :]

    rt = r * p_in
    at = -kk * p_ex
    kt = k * p_inv
    bt = kk * a * p_inv
    kh = kt * p_last
    bh = bt * p_last

    per_tile = LANES // D
    lane = lax.broadcasted_iota(jnp.int32, (1, LANES), 1) // D
    hmb = [(lane == h).astype(BF16) for h in range(per_tile)]
    rowi = lax.broadcasted_iota(jnp.int32, (L, QW), 0)
    coli = lax.broadcasted_iota(jnp.int32, (L, QW), 1) % D
    strict = coli < rowi
    lower = coli <= rowi
    eye = (coli == rowi).astype(F32)
    zero_tile = jnp.zeros((L, LANES), BF16)

    def head_rows(mb):
        blocks = []
        for h in range(HQ):
            t = h // per_tile
            keep = mb[:, t * LANES:(t + 1) * LANES] * hmb[h % per_tile]
            blocks.append(jnp.concatenate(
                [keep if c == t else zero_tile for c in range(QW // LANES)], axis=1))
        return blocks

    def bd(m):
        return jnp.concatenate(head_rows(m.astype(BF16)), axis=0)

    def cat2(x, y):
        return jnp.concatenate([x, y], axis=0)

    xs, vs, a_ab, a_ak, a_rb, a_rk = [], [], [], [], [], []
    for sl in qs:
        xq = cat2(at[:, sl], rt[:, sl]).astype(BF16)
        ycat = jnp.concatenate(head_rows(bt[:, sl].astype(BF16))
                               + head_rows(kt[:, sl].astype(BF16)), axis=0)
        aall = lax.dot_general(xq, ycat, (((1,), (1,)), ((), ())),
                               preferred_element_type=F32)
        xs.append(xq)
        vs.append(v[:, sl])
        a_ab.append(jnp.where(strict, aall[0:L, 0:QW], 0.0))
        a_ak.append(jnp.where(strict, aall[0:L, QW:2 * QW], 0.0))
        a_rb.append(jnp.where(lower, aall[L:2 * L, 0:QW], 0.0))
        a_rk.append(jnp.where(lower, aall[L:2 * L, QW:2 * QW], 0.0))

    tinv = [eye + m for m in a_ab]
    pw = [_bdot(m.astype(BF16), bd(m)) for m in a_ab]
    sx = [_bdot(xs[q], st_s[q].astype(BF16)) for q in range(NQ)]
    av = [_bdot(cat2(a_ak[q], a_rk[q]).astype(BF16), bd(vs[q])) for q in range(NQ)]
    for _ in range(L.bit_length() - 3):
        both = [_bdot(cat2(tinv[q], pw[q]).astype(BF16), bd(pw[q])) for q in range(NQ)]
        tinv = [tinv[q] + both[q][0:L] for q in range(NQ)]
        pw = [both[q][L:2 * L] for q in range(NQ)]
    tinv = [tinv[q] + _bdot(tinv[q].astype(BF16), bd(pw[q])) for q in range(NQ)]

    us = [_bdot(tinv[q].astype(BF16), bd(sx[q][0:L] + av[q][0:L])) for q in range(NQ)]
    ys = [sx[q][L:2 * L] + av[q][L:2 * L] + _bdot(a_rb[q].astype(BF16), bd(us[q]))
          for q in range(NQ)]

    for q, sl in enumerate(qs):
        upd = lax.dot_general(cat2(bh[:, sl], kh[:, sl]).astype(BF16),
                              cat2(us[q], vs[q]).astype(BF16), (((0,), (0,)), ((), ())),
                              preferred_element_type=F32)
        cols = []
        for cb in range(QW // LANES):
            lo_l = q * QW + cb * LANES
            cols.append(jnp.broadcast_to(p_last[:, lo_l:lo_l + LANES], (LANES, LANES)).T)
        pcol = jnp.concatenate(cols, axis=0)
        pcol = jnp.concatenate([pcol] * (QW // LANES), axis=1)
        st_s[q] = pcol * st_s[q] + jnp.where(seg, upd, 0.0)

    yv = jnp.concatenate(ys, axis=1)
    cen = yv - segsum(yv) * (1.0 / D)
    var = segsum(cen * cen) * (1.0 / D)
    y = cen * lax.rsqrt(var + RWKV_GN_EPS) * gg_ref[...] + gb_ref[...] + bonus
    o_ref[...] = (y * _silu(z_ref[...].astype(F32))).astype(BF16)


def _l1_rwkv(uf, uz, lo, p, batch, seq, width):
    L = RWKV_CHUNK
    QW = RWKV_QUAD_WIDTH
    wt = RWKV_QUADS_PER_STEP * QW
    ngrp = width // wt
    nst = seq // L
    vec = lambda: pl.BlockSpec((1, wt), lambda b, g, s: (0, g))
    col = lambda off: pl.BlockSpec((L, wt), lambda b, g, s: (b * nst + s, off + g))
    return pl.pallas_call(
        _rwkv_kernel,
        grid=(batch, ngrp, nst),
        in_specs=[col(0), col(ngrp), col(2 * ngrp), col(0),
                  pl.BlockSpec((L, 2 * LORA_PAD), lambda b, g, s: (b * nst + s, 0)),
                  vec(),
                  pl.BlockSpec((LORA_PAD, wt), lambda b, g, s: (0, g)),
                  vec(),
                  pl.BlockSpec((LORA_PAD, wt), lambda b, g, s: (0, g)),
                  vec(), vec(), vec(), vec(), vec()],
        out_specs=pl.BlockSpec((L, wt), lambda b, g, s: (b * nst + s, g)),
        out_shape=jax.ShapeDtypeStruct((batch * seq, width), BF16),
        scratch_shapes=[pltpu.VMEM((RWKV_QUADS_PER_STEP, QW, QW), F32)],
        compiler_params=_params(("arbitrary", "arbitrary", "arbitrary")),
        name="l1_rwkv",
    )(uf, uf, uf, uz, lo, p["w0"], p["w2"], p["a0"], p["a2"],
      p["kk"], p["ka"], p["rk"], p["gg"], p["gb"])


def _bd4_dense(w):
    nb, bs, _ = w.shape
    per = MXU_DIM // bs
    wb = w.reshape(nb // per, per, bs, bs)
    dense = jnp.einsum('cnij,nm->cnimj', wb, jnp.eye(per, dtype=w.dtype))
    return dense.reshape(nb // per, MXU_DIM, MXU_DIM).astype(BF16)


def _row(v):
    return v.reshape(1, -1).astype(F32)


def _pad_rows(w, rows):
    return jnp.pad(w, ((0, rows - w.shape[0]), (0, 0)))


def _pad_cols(w, cols):
    return jnp.pad(w, ((0, 0), (0, cols - w.shape[1])))


def kernel(x, l0_norm_pre, l0_w_in, l0_lru_conv_w, l0_lru_conv_b, l0_lru_wa, l0_lru_ba,
           l0_lru_wx, l0_lru_bx, l0_lru_lambda, l0_m_conv_w, l0_m_conv_b, l0_m_wq, l0_m_wk,
           l0_m_wv, l0_m_wi, l0_m_bi, l0_m_wf, l0_m_bf, l0_m_skip, l0_m_gn, l0_w_out,
           l0_norm_post, l1_norm_pre, l1_w_in, l1_mu_rkv, l1_mu_w, l1_mu_a, l1_w0, l1_w1,
           l1_w2, l1_a0, l1_a1, l1_a2, l1_k_k, l1_k_a, l1_r_k, l1_gn_g, l1_gn_b, l1_w_out,
           l1_norm_post):
    batch, seq, d = x.shape
    x2 = x.reshape(batch * seq, d).astype(F32)
    rw = l0_lru_lambda.shape[0]
    mw = l0_m_skip.shape[0]
    ww = l1_w0.shape[0]
    assert rw == mw, "column blocking assumes equal head-group widths"

    w_in0 = jnp.concatenate([l0_w_in[:, 0:rw], l0_w_in[:, 2 * rw:2 * rw + mw],
                             l0_w_in[:, rw:2 * rw], l0_w_in[:, 2 * rw + mw:]], axis=1).astype(BF16)
    uf0, uz0 = _l0_in(x2, _row(l0_norm_pre), w_in0, rw + mw)

    lru_p = dict(cw=l0_lru_conv_w.astype(F32), cb=_row(l0_lru_conv_b),
                 wa=l0_lru_wa.astype(BF16), ba=_row(l0_lru_ba),
                 wx=l0_lru_wx.astype(BF16), bx=_row(l0_lru_bx), lam=_row(l0_lru_lambda))
    yr = _l0_lru(uf0, uz0, lru_p, batch, seq, rw)

    wg = jnp.concatenate([l0_m_wi, l0_m_wf], axis=1)
    wg = _pad_cols(wg, LANES).reshape(3, mw, LANES).astype(BF16)
    bg = _pad_cols(jnp.concatenate([l0_m_bi, l0_m_bf]).reshape(1, -1), LANES).astype(F32)
    m_p = dict(cw=l0_m_conv_w.astype(F32), cb=_row(l0_m_conv_b),
               wq=_bd4_dense(l0_m_wq), wk=_bd4_dense(l0_m_wk),
               wv=_bd4_dense(l0_m_wv), wg=wg, bg=bg,
               skip=_row(l0_m_skip), gn=_row(l0_m_gn))
    ym = _l0_mlstm(uf0, uz0, m_p, batch, seq, mw, x_col=rw // mw, z_col=rw // mw)

    w_out0 = l0_w_out.astype(BF16).reshape(2, rw, d)
    x1 = _out_proj([yr, ym], w_out0, _row(l0_norm_post), x2)

    w1 = _pad_cols(l1_w1, LORA_PAD).astype(BF16)
    a1 = _pad_cols(l1_a1, LORA_PAD).astype(BF16)
    uf1, uz1, lo1 = _l1_in(x1, _row(l1_norm_pre), l1_w_in.astype(BF16), _row(l1_mu_w),
                           _row(l1_mu_a), w1, a1, _row(l1_mu_rkv), seq)
    r_p = dict(w0=_row(l1_w0),
               w2=_pad_rows(l1_w2, LORA_PAD).astype(BF16), a0=_row(l1_a0),
               a2=_pad_rows(l1_a2, LORA_PAD).astype(BF16), kk=_row(l1_k_k), ka=_row(l1_k_a),
               rk=_row(l1_r_k), gg=_row(l1_gn_g), gb=_row(l1_gn_b))
    y1 = _l1_rwkv(uf1, uz1, lo1, r_p, batch, seq, ww)
    out = _out_proj([y1], l1_w_out.astype(BF16).reshape(1, ww, d), _row(l1_norm_post), x1)
    return out.reshape(batch, seq, d).astype(x.dtype)
```

```python
import functools

import jax
import jax.numpy as jnp
from jax import lax
from jax.experimental import pallas as pl
from jax.experimental.pallas import tpu as pltpu

F32 = jnp.float32
BF16 = jnp.bfloat16

RMS_EPS = 1e-6
CONV_WIDTH = 4
LRU_C = 8.0
LRU_BLOCK = 128
MLSTM_HEADS = 8
MLSTM_QKV_BLOCK = 4
MLSTM_CHUNK = 128
MLSTM_GN_EPS = 1e-6
RWKV_HEAD_DIM = 64
RWKV_GN_EPS = 64e-5
RWKV_CHUNK = 64
RWKV_DECAY_SCALE = 0.6065306597126334
RWKV_QUAD_WIDTH = 256
RWKV_QUADS_PER_STEP = 16
LORA_PAD = 128

LANES = 128
SUBLANES = 8
MXU_DIM = 256
VMEM_LIMIT = 56 * 1024 * 1024


def _params(sem):
    return pltpu.CompilerParams(dimension_semantics=sem, vmem_limit_bytes=VMEM_LIMIT)


def _sigmoid(x):
    return 1.0 / (1.0 + jnp.exp(-x))


def _silu(x):
    return x * _sigmoid(x)


def _softplus(x):
    return jnp.maximum(x, 0.0) + jnp.log1p(jnp.exp(-jnp.abs(x)))


def _dot(a, b):
    return jnp.dot(a.astype(BF16), b.astype(BF16), preferred_element_type=F32)


def _dot_nt(a, b):
    return lax.dot_general(a.astype(BF16), b.astype(BF16), (((1,), (1,)), ((), ())),
                           preferred_element_type=F32)


def _dot_tn(a, b):
    return lax.dot_general(a.astype(BF16), b.astype(BF16), (((0,), (0,)), ((), ())),
                           preferred_element_type=F32)


def _split3(x):
    hi = x.astype(BF16)
    r1 = x - hi.astype(F32)
    mid = r1.astype(BF16)
    lo = (r1 - mid.astype(F32)).astype(BF16)
    return hi, mid, lo


def _dot_exact_rhs(a_f32, b_bf16):
    hi, mid, lo = _split3(a_f32)
    out = jnp.dot(hi, b_bf16, preferred_element_type=F32)
    out = out + jnp.dot(mid, b_bf16, preferred_element_type=F32)
    return out + jnp.dot(lo, b_bf16, preferred_element_type=F32)


def _dot_exact_lhs(a_bf16, b_f32):
    hi, mid, lo = _split3(b_f32)
    out = jnp.dot(a_bf16, hi, preferred_element_type=F32)
    out = out + jnp.dot(a_bf16, mid, preferred_element_type=F32)
    return out + jnp.dot(a_bf16, lo, preferred_element_type=F32)


def _rms_norm(x, g):
    return x * lax.rsqrt(jnp.mean(x * x, axis=-1, keepdims=True) + RMS_EPS) * g


def _tri(n, inclusive=True):
    r = lax.broadcasted_iota(jnp.int32, (n, n), 0)
    c = lax.broadcasted_iota(jnp.int32, (n, n), 1)
    return (c <= r) if inclusive else (c < r)


def _l0_in_kernel(x_ref, g_ref, w_ref, of_ref, oz_ref, h_ref, *, nf):
    j = pl.program_id(1)

    @pl.when(j == 0)
    def _():
        h_ref[...] = _rms_norm(x_ref[...], g_ref[...]).astype(BF16)

    acc = jnp.dot(h_ref[...], w_ref[...], preferred_element_type=F32)

    @pl.when(j < nf)
    def _():
        of_ref[...] = acc

    @pl.when(j >= nf)
    def _():
        oz_ref[...] = acc.astype(BF16)


def _l0_in(x2, g, w_bf16, n_f32, tm=1024, tn=1024):
    t, d = x2.shape
    n = w_bf16.shape[1]
    nf = n_f32 // tn
    nz = (n - n_f32) // tn
    return pl.pallas_call(
        functools.partial(_l0_in_kernel, nf=nf),
        grid=(t // tm, nf + nz),
        in_specs=[pl.BlockSpec((tm, d), lambda i, j: (i, 0)),
                  pl.BlockSpec((1, d), lambda i, j: (0, 0)),
                  pl.BlockSpec((d, tn), lambda i, j: (0, j))],
        out_specs=[pl.BlockSpec((tm, tn), lambda i, j: (i, jnp.minimum(j, nf - 1))),
                   pl.BlockSpec((tm, tn), lambda i, j: (i, jnp.maximum(j - nf, 0)))],
        out_shape=[jax.ShapeDtypeStruct((t, n_f32), F32),
                   jax.ShapeDtypeStruct((t, n - n_f32), BF16)],
        scratch_shapes=[pltpu.VMEM((tm, d), BF16)],
        compiler_params=_params(("arbitrary", "arbitrary")),
        name="l0_in",
    )(x2, g, w_bf16)


def _causal_conv(x, tail_ref, w, b):
    ts = x.shape[0]
    row = lax.broadcasted_iota(jnp.int32, (SUBLANES, x.shape[1]), 0)
    tail = tail_ref[...]
    y = b + x * w[CONV_WIDTH - 1:CONV_WIDTH, :]
    for k in range(1, CONV_WIDTH):
        rolled = pltpu.roll(x, k, 0)
        top = jnp.where(row < k, pltpu.roll(tail, k, 0), rolled[0:SUBLANES])
        xk = jnp.concatenate([top, rolled[SUBLANES:]], axis=0)
        y = y + xk * w[CONV_WIDTH - 1 - k:CONV_WIDTH - k, :]
    tail_ref[...] = x[ts - SUBLANES:ts]
    return y


def _lru_kernel(x_ref, z_ref, cw_ref, cb_ref, wa_ref, ba_ref, wx_ref, bx_ref, lam_ref, o_ref,
                xbuf, a_p, u_p, hl_s, al_s, hc_s, *, ts, nblk, pitch):
    s = pl.program_id(2)

    @pl.when(s == 0)
    def _():
        xbuf[...] = jnp.zeros_like(xbuf)
        hc_s[...] = jnp.zeros_like(hc_s)

    xc = _causal_conv(x_ref[...], xbuf, cw_ref[...], cb_ref[...])

    gr, gi = [], []
    for n in range(nblk):
        xn = xc[:, n * LRU_BLOCK:(n + 1) * LRU_BLOCK]
        gr.append(_dot(xn, wa_ref[n]))
        gi.append(_dot(xn, wx_ref[n]))
    r = _sigmoid(jnp.concatenate(gr, axis=1) + ba_ref[...])
    i = _sigmoid(jnp.concatenate(gi, axis=1) + bx_ref[...])
    log_a = -LRU_C * r * _softplus(-lam_ref[...])
    a = jnp.exp(log_a)
    u = jnp.sqrt(-jnp.tanh(log_a) * (a * a + 1.0)) * (i * xc)

    seg = ts // SUBLANES
    nslab = a.shape[1] // LANES
    for c in range(nslab):
        for j in range(SUBLANES):
            a_p[c, pl.ds(j * pitch, seg), :] = a[j * seg:(j + 1) * seg, c * LANES:(c + 1) * LANES]
            u_p[c, pl.ds(j * pitch, seg), :] = u[j * seg:(j + 1) * seg, c * LANES:(c + 1) * LANES]

    def local_scan(t, carry):
        out = []
        for c in range(nslab):
            h, acc = carry[c]
            at = a_p[c, pl.ds(t, SUBLANES, stride=pitch), :]
            ut = u_p[c, pl.ds(t, SUBLANES, stride=pitch), :]
            h = at * h + ut
            acc = at * acc
            hl_s[c, t] = h
            al_s[c, t] = acc
            out.append((h, acc))
        return tuple(out)

    init = tuple((jnp.zeros((SUBLANES, LANES), F32), jnp.ones((SUBLANES, LANES), F32))
                 for _ in range(nslab))
    ends = lax.fori_loop(0, seg, local_scan, init, unroll=4)

    row = lax.broadcasted_iota(jnp.int32, (SUBLANES, LANES), 0)
    for c in range(nslab):
        hs, ac = ends[c]
        for sft in (1, 2, 4):
            a_sh = jnp.where(row >= sft, pltpu.roll(ac, sft, 0), 1.0)
            h_sh = jnp.where(row >= sft, pltpu.roll(hs, sft, 0), 0.0)
            hs = ac * h_sh + hs
            ac = ac * a_sh
        carry = hc_s[:, c * LANES:(c + 1) * LANES]
        h_end = ac * carry + hs
        h_in = jnp.where(row == 0, carry, pltpu.roll(h_end, 1, 0))
        hc_s[:, c * LANES:(c + 1) * LANES] = h_end[SUBLANES - 1:SUBLANES, :]

        def fix(t, _, c=c, h_in=h_in):
            u_p[c, pl.ds(t, SUBLANES, stride=pitch), :] = hl_s[c, t] + al_s[c, t] * h_in
            return 0

        lax.fori_loop(0, seg, fix, 0, unroll=4)

    for c in range(nslab):
        for j in range(SUBLANES):
            rows = slice(j * seg, (j + 1) * seg)
            lanes = slice(c * LANES, (c + 1) * LANES)
            hj = u_p[c, pl.ds(j * pitch, seg), :]
            o_ref[rows, lanes] = (hj * _silu(z_ref[rows, lanes].astype(F32))).astype(BF16)


def _l0_lru(uf, uz, p, batch, seq, width, ts=512, tc=512):
    nblk = tc // LRU_BLOCK
    nct = width // tc
    nst = seq // ts
    seg = ts // SUBLANES
    pitch = seg + SUBLANES
    assert seg % (2 * SUBLANES) == 0
    nslab = tc // LANES
    vec = lambda: pl.BlockSpec((1, tc), lambda b, c, s: (0, c))
    return pl.pallas_call(
        functools.partial(_lru_kernel, ts=ts, nblk=nblk, pitch=pitch),
        grid=(batch, nct, nst),
        in_specs=[pl.BlockSpec((ts, tc), lambda b, c, s: (b * nst + s, c)),
                  pl.BlockSpec((ts, tc), lambda b, c, s: (b * nst + s, c)),
                  pl.BlockSpec((CONV_WIDTH, tc), lambda b, c, s: (0, c)),
                  vec(),
                  pl.BlockSpec((nblk, LRU_BLOCK, LRU_BLOCK), lambda b, c, s: (c, 0, 0)),
                  vec(),
                  pl.BlockSpec((nblk, LRU_BLOCK, LRU_BLOCK), lambda b, c, s: (c, 0, 0)),
                  vec(), vec()],
        out_specs=pl.BlockSpec((ts, tc), lambda b, c, s: (b * nst + s, c)),
        out_shape=jax.ShapeDtypeStruct((batch * seq, width), BF16),
        scratch_shapes=[pltpu.VMEM((SUBLANES, tc), F32),
                        pltpu.VMEM((nslab, SUBLANES * pitch, LANES), F32),
                        pltpu.VMEM((nslab, SUBLANES * pitch, LANES), F32),
                        pltpu.VMEM((nslab, seg, SUBLANES, LANES), F32),
                        pltpu.VMEM((nslab, seg, SUBLANES, LANES), F32),
                        pltpu.VMEM((1, tc), F32)],
        compiler_params=_params(("arbitrary", "arbitrary", "arbitrary")),
        name="l0_lru",
    )(uf, uz, p["cw"], p["cb"], p["wa"], p["ba"], p["wx"], p["bx"], p["lam"])


def _bd4(xb, w_ref):
    nb, bw, _ = w_ref.shape
    return jnp.concatenate(
        [jnp.dot(xb[:, c * bw:(c + 1) * bw], w_ref[c], preferred_element_type=F32)
         for c in range(nb)], axis=1)


def _mlstm_kernel(x_ref, z_ref, cw_ref, cb_ref, wq_ref, wk_ref, wv_ref, wg_ref, bg_ref,
                  skip_ref, gn_ref, o_ref, xbuf, c_s, n_s, m_s, *, hd):
    L = MLSTM_CHUNK
    H = MLSTM_HEADS
    s = pl.program_id(1)

    @pl.when(s == 0)
    def _():
        xbuf[...] = jnp.zeros_like(xbuf)
        c_s[...] = jnp.zeros_like(c_s)
        n_s[...] = jnp.zeros_like(n_s)
        m_s[...] = jnp.zeros_like(m_s)

    x = x_ref[...]
    xmc = _silu(_causal_conv(x, xbuf, cw_ref[...], cb_ref[...]))

    xmc_b = xmc.astype(BF16)
    q = _bd4(xmc_b, wq_ref)
    k = _bd4(xmc_b, wk_ref)
    v = _bd4(x.astype(BF16), wv_ref)

    g = _dot(q, wg_ref[0]) + _dot(k, wg_ref[1]) + _dot(v, wg_ref[2]) + bg_ref[...]
    lane = lax.broadcasted_iota(jnp.int32, (L, LANES), 1)
    lf = jnp.minimum(g, 0.0) - jnp.log1p(jnp.exp(-jnp.abs(g)))
    lf = jnp.where((lane >= H) & (lane < 2 * H), lf, 0.0)
    csum = _dot_exact_lhs(_tri(L).astype(BF16), lf)
    gb = jnp.where(lane < H, g, csum)
    gbt = gb.T

    causal = _tri(L)
    kscale = hd ** -0.5
    heads = range(H)
    sls = [slice(h * hd, (h + 1) * hd) for h in heads]
    qb = [q[:, sl].astype(BF16) for sl in sls]
    kh = [k[:, sl] * kscale for sl in sls]
    vb = [v[:, sl].astype(BF16) for sl in sls]
    s_raw = [lax.dot_general(qb[h], kh[h].astype(BF16), (((1,), (1,)), ((), ())),
                             preferred_element_type=F32) for h in heads]
    qc = [jnp.dot(qb[h], c_s[h].astype(BF16), preferred_element_type=F32) for h in heads]

    wgt, inter, m_t, decay, kw, m_new = [], [], [], [], [], []
    for h in heads:
        b_col = gb[:, H + h:H + h + 1]
        i_col = gb[:, h:h + 1]
        b_row = gbt[H + h:H + h + 1, :]
        i_row = gbt[h:h + 1, :]
        m_prev = m_s[h][0:1, 0:1]
        dm = jnp.where(causal, b_col - b_row + i_row, -1e30)
        m_inter = b_col + m_prev
        mt = jnp.maximum(jnp.max(dm, axis=-1, keepdims=True), m_inter)
        wgt.append(jnp.exp(dm - mt))
        inter.append(jnp.exp(m_inter - mt))
        m_t.append(mt)
        b_last = b_col[L - 1:L, :]
        g_row = b_last - b_row + i_row
        g_col = b_last - b_col + i_col
        mn = jnp.maximum(b_last + m_prev, jnp.max(g_row, axis=-1, keepdims=True))
        m_new.append(mn)
        decay.append(jnp.exp(b_last + m_prev - mn))
        kw.append(kh[h] * jnp.exp(g_col - mn))

    upd = [lax.dot_general(kw[h].astype(BF16), vb[h], (((0,), (0,)), ((), ())),
                           preferred_element_type=F32) for h in heads]
    scores = [s_raw[h] * wgt[h] for h in heads]
    num = [jnp.dot(scores[h].astype(BF16), vb[h], preferred_element_type=F32)
           + inter[h] * qc[h] for h in heads]

    ys = []
    for h in heads:
        nvec = n_s[h]
        den = (jnp.sum(scores[h], axis=-1, keepdims=True)
               + inter[h] * jnp.sum(q[:, sls[h]] * nvec, axis=-1, keepdims=True))
        hout = num[h] / jnp.maximum(jnp.abs(den), jnp.exp(-m_t[h]))
        c_s[h] = decay[h] * c_s[h] + upd[h]
        n_s[h] = decay[h] * nvec + jnp.sum(kw[h], axis=0, keepdims=True)
        m_s[h] = jnp.broadcast_to(m_new[h], (SUBLANES, LANES))
        mu = jnp.mean(hout, axis=-1, keepdims=True)
        cen = hout - mu
        var = jnp.mean(cen * cen, axis=-1, keepdims=True)
        ys.append(cen * lax.rsqrt(var + MLSTM_GN_EPS))
    y = jnp.concatenate(ys, axis=1) * gn_ref[...] + skip_ref[...] * xmc
    o_ref[...] = (y * _silu(z_ref[...].astype(F32))).astype(BF16)


def _l0_mlstm(uf, uz, p, batch, seq, width, x_col, z_col):
    L = MLSTM_CHUNK
    nst = seq // L
    hd = width // MLSTM_HEADS
    full = lambda shape: pl.BlockSpec(shape, lambda b, s: (0,) * len(shape))
    return pl.pallas_call(
        functools.partial(_mlstm_kernel, hd=hd),
        grid=(batch, nst),
        in_specs=[pl.BlockSpec((L, width), lambda b, s: (b * nst + s, x_col)),
                  pl.BlockSpec((L, width), lambda b, s: (b * nst + s, z_col)),
                  full((CONV_WIDTH, width)), full((1, width)),
                  full((width // MXU_DIM, MXU_DIM, MXU_DIM)),
                  full((width // MXU_DIM, MXU_DIM, MXU_DIM)),
                  full((width // MXU_DIM, MXU_DIM, MXU_DIM)),
                  full((3, width, LANES)), full((1, LANES)),
                  full((1, width)), full((1, width))],
        out_specs=pl.BlockSpec((L, width), lambda b, s: (b * nst + s, 0)),
        out_shape=jax.ShapeDtypeStruct((batch * seq, width), BF16),
        scratch_shapes=[pltpu.VMEM((SUBLANES, width), F32),
                        pltpu.VMEM((MLSTM_HEADS, hd, hd), F32),
                        pltpu.VMEM((MLSTM_HEADS, 1, hd), F32),
                        pltpu.VMEM((MLSTM_HEADS, SUBLANES, LANES), F32)],
        compiler_params=_params(("arbitrary", "arbitrary")),
        name="l0_mlstm",
    )(uf, uz, p["cw"], p["cb"], p["wq"], p["wk"], p["wv"], p["wg"], p["bg"], p["skip"], p["gn"])


def _out_kernel(*refs, nin):
    y_refs = refs[:nin]
    w_ref, g_ref, x_ref, o_ref = refs[nin:]
    acc = jnp.dot(y_refs[0][...], w_ref[0], preferred_element_type=F32)
    for n in range(1, nin):
        acc = acc + jnp.dot(y_refs[n][...], w_ref[n], preferred_element_type=F32)
    o_ref[...] = x_ref[...] + _rms_norm(acc, g_ref[...])


def _out_proj(ys, w_bf16, g, x2, tm=256):
    nin, kin, d = w_bf16.shape
    t = x2.shape[0]
    return pl.pallas_call(
        functools.partial(_out_kernel, nin=nin),
        grid=(t // tm,),
        in_specs=[pl.BlockSpec((tm, kin), lambda i: (i, 0)) for _ in range(nin)]
        + [pl.BlockSpec((nin, kin, d), lambda i: (0, 0, 0), pipeline_mode=pl.Buffered(1)),
           pl.BlockSpec((1, d), lambda i: (0, 0)),
           pl.BlockSpec((tm, d), lambda i: (i, 0))],
        out_specs=pl.BlockSpec((tm, d), lambda i: (i, 0)),
        out_shape=jax.ShapeDtypeStruct((t, d), F32),
        compiler_params=_params(("arbitrary",)),
        name="out_proj",
    )(*ys, w_bf16, g, x2)


def _shift_rows(x, last_ref, first):
    row = lax.broadcasted_iota(jnp.int32, x.shape, 0)
    last = jnp.where(first, 0.0, last_ref[...])
    prev = jnp.where(row == 0, last, pltpu.roll(x, 1, 0))
    last_ref[...] = x[x.shape[0] - 1:x.shape[0], :]
    return prev


def _l1_in_kernel(x_ref, g_ref, w_ref, muw_ref, mua_ref, w1_ref, a1_ref, mu_ref, of_ref, oz_ref,
                  lo_ref, h_ref, last_ref, lastu_ref, *, nf, tiles_per_seq):
    i = pl.program_id(0)
    j = pl.program_id(1)
    first = i % tiles_per_seq == 0

    @pl.when(j == 0)
    def _():
        h = _rms_norm(x_ref[...], g_ref[...])
        h_ref[...] = h.astype(BF16)
        dh = _shift_rows(h, last_ref, first) - h
        xw = h + dh * muw_ref[...]
        xa = h + dh * mua_ref[...]
        lo_ref[:, 0:LORA_PAD] = jnp.tanh(_dot(xw, w1_ref[...]))
        lo_ref[:, LORA_PAD:2 * LORA_PAD] = _dot(xa, a1_ref[...])

    acc = jnp.dot(h_ref[...], w_ref[...], preferred_element_type=F32)
    mixed = acc + (_shift_rows(acc, lastu_ref.at[j], first) - acc) * mu_ref[...]

    @pl.when(j < nf)
    def _():
        of_ref[...] = mixed

    @pl.when(j >= nf)
    def _():
        oz_ref[...] = mixed.astype(BF16)


def _l1_in(x2, g, w_bf16, muw, mua, w1, a1, mu_rkv, seq, tm=1024, tn=1024):
    t, d = x2.shape
    n = w_bf16.shape[1]
    n_f32 = mu_rkv.shape[1]
    nf = n_f32 // tn
    nz = (n - n_f32) // tn
    mu_rkv = _pad_cols(mu_rkv, n)
    c2 = lambda shape: pl.BlockSpec(shape, lambda i, j: (0, 0))
    return pl.pallas_call(
        functools.partial(_l1_in_kernel, nf=nf, tiles_per_seq=seq // tm),
        grid=(t // tm, nf + nz),
        in_specs=[pl.BlockSpec((tm, d), lambda i, j: (i, 0)),
                  c2((1, d)),
                  pl.BlockSpec((d, tn), lambda i, j: (0, j)),
                  c2((1, d)), c2((1, d)), c2((d, LORA_PAD)), c2((d, LORA_PAD)),
                  pl.BlockSpec((1, tn), lambda i, j: (0, j))],
        out_specs=[pl.BlockSpec((tm, tn), lambda i, j: (i, jnp.minimum(j, nf - 1))),
                   pl.BlockSpec((tm, tn), lambda i, j: (i, jnp.maximum(j - nf, 0))),
                   pl.BlockSpec((tm, 2 * LORA_PAD), lambda i, j: (i, 0))],
        out_shape=[jax.ShapeDtypeStruct((t, n_f32), F32),
                   jax.ShapeDtypeStruct((t, n - n_f32), BF16),
                   jax.ShapeDtypeStruct((t, 2 * LORA_PAD), F32)],
        scratch_shapes=[pltpu.VMEM((tm, d), BF16), pltpu.VMEM((1, d), F32),
                        pltpu.VMEM((nf + nz, 1, tn), F32)],
        compiler_params=_params(("arbitrary", "arbitrary")),
        name="l1_in",
    )(x2, g, w_bf16, muw, mua, w1, a1, mu_rkv)


def _bdot(a, b):
    return jnp.dot(a, b, preferred_element_type=F32)


def _rwkv_kernel(r_ref, k_ref, v_ref, z_ref, lo_ref, w0_ref, w2_ref,
                 a0_ref, a2_ref, kk_ref, ka_ref, rk_ref, gg_ref, gb_ref, o_ref, st_s):
    L = RWKV_CHUNK
    D = RWKV_HEAD_DIM
    QW = RWKV_QUAD_WIDTH
    HQ = QW // D
    NQ = r_ref.shape[1] // QW
    s = pl.program_id(2)

    @pl.when(s == 0)
    def _():
        st_s[...] = jnp.zeros_like(st_s)

    r = r_ref[...]
    k = k_ref[...]
    v = v_ref[...]

    lo = lo_ref[...]
    logw = -RWKV_DECAY_SCALE * _sigmoid(w0_ref[...] + _dot(lo[:, 0:LORA_PAD], w2_ref[...]))
    a = _sigmoid(a0_ref[...] + _dot(lo[:, LORA_PAD:2 * LORA_PAD], a2_ref[...]))

    seg = ((lax.broadcasted_iota(jnp.int32, (QW, QW), 0) // D)
           == (lax.broadcasted_iota(jnp.int32, (QW, QW), 1) // D))
    seg_b = seg.astype(BF16)
    qs = [slice(q * QW, (q + 1) * QW) for q in range(NQ)]

    def segsum(x):
        return jnp.concatenate([_bdot(x[:, sl].astype(BF16), seg_b) for sl in qs], axis=1)

    kk = k * kk_ref[...]
    kk = kk * lax.rsqrt(jnp.maximum(segsum(kk * kk), 1e-24))
    k = k * (1.0 + (a - 1.0) * ka_ref[...])
    bonus = segsum(r * k * rk_ref[...]) * v

    tri_b = _tri(L).astype(BF16)
    lw_hi = logw.astype(BF16)
    lw_lo = (logw - lw_hi.astype(F32)).astype(BF16)
    cs = _bdot(tri_b, lw_hi) + _bdot(tri_b, lw_lo)
    p_in = jnp.exp(cs)
    p_inv = jnp.exp(-cs)
    p_ex = jnp.exp(cs - logw)
    p_last = p_in[L - 1:L, :]

    rt = r * p_in
    at = -kk * p_ex
    kt = k * p_inv
    bt = kk * a * p_inv
    kh = kt * p_last
    bh = bt * p_last

    per_tile = LANES // D
    lane = lax.broadcasted_iota(jnp.int32, (1, LANES), 1) // D
    hmb = [(lane == h).astype(BF16) for h in range(per_tile)]
    rowi = lax.broadcasted_iota(jnp.int32, (L, QW), 0)
    coli = lax.broadcasted_iota(jnp.int32, (L, QW), 1) % D
    strict = coli < rowi
    lower = coli <= rowi
    eye = (coli == rowi).astype(F32)
    zero_tile = jnp.zeros((L, LANES), BF16)

    def head_rows(mb):
        blocks = []
        for h in range(HQ):
            t = h // per_tile
            keep = mb[:, t * LANES:(t + 1) * LANES] * hmb[h % per_tile]
            blocks.append(jnp.concatenate(
                [keep if c == t else zero_tile for c in range(QW // LANES)], axis=1))
        return blocks

    def bd(m):
        return jnp.concatenate(head_rows(m.astype(BF16)), axis=0)

    def cat2(x, y):
        return jnp.concatenate([x, y], axis=0)

    xs, vs, a_ab, a_ak, a_rb, a_rk = [], [], [], [], [], []
    for sl in qs:
        xq = cat2(at[:, sl], rt[:, sl]).astype(BF16)
        ycat = jnp.concatenate(head_rows(bt[:, sl].astype(BF16))
                               + head_rows(kt[:, sl].astype(BF16)), axis=0)
        aall = lax.dot_general(xq, ycat, (((1,), (1,)), ((), ())),
                               preferred_element_type=F32)
        xs.append(xq)
        vs.append(v[:, sl])
        a_ab.append(jnp.where(strict, aall[0:L, 0:QW], 0.0))
        a_ak.append(jnp.where(strict, aall[0:L, QW:2 * QW], 0.0))
        a_rb.append(jnp.where(lower, aall[L:2 * L, 0:QW], 0.0))
        a_rk.append(jnp.where(lower, aall[L:2 * L, QW:2 * QW], 0.0))

    tinv = [eye + m for m in a_ab]
    pw = [_bdot(m.astype(BF16), bd(m)) for m in a_ab]
    sx = [_bdot(xs[q], st_s[q].astype(BF16)) for q in range(NQ)]
    av = [_bdot(cat2(a_ak[q], a_rk[q]).astype(BF16), bd(vs[q])) for q in range(NQ)]
    for _ in range(L.bit_length() - 3):
        both = [_bdot(cat2(tinv[q], pw[q]).astype(BF16), bd(pw[q])) for q in range(NQ)]
        tinv = [tinv[q] + both[q][0:L] for q in range(NQ)]
        pw = [both[q][L:2 * L] for q in range(NQ)]
    tinv = [tinv[q] + _bdot(tinv[q].astype(BF16), bd(pw[q])) for q in range(NQ)]

    us = [_bdot(tinv[q].astype(BF16), bd(sx[q][0:L] + av[q][0:L])) for q in range(NQ)]
    ys = [sx[q][L:2 * L] + av[q][L:2 * L] + _bdot(a_rb[q].astype(BF16), bd(us[q]))
          for q in range(NQ)]

    for q, sl in enumerate(qs):
        upd = lax.dot_general(cat2(bh[:, sl], kh[:, sl]).astype(BF16),
                              cat2(us[q], vs[q]).astype(BF16), (((0,), (0,)), ((), ())),
                              preferred_element_type=F32)
        cols = []
        for cb in range(QW // LANES):
            lo_l = q * QW + cb * LANES
            cols.append(jnp.broadcast_to(p_last[:, lo_l:lo_l + LANES], (LANES, LANES)).T)
        pcol = jnp.concatenate(cols, axis=0)
        pcol = jnp.concatenate([pcol] * (QW // LANES), axis=1)
        st_s[q] = pcol * st_s[q] + jnp.where(seg, upd, 0.0)

    yv = jnp.concatenate(ys, axis=1)
    cen = yv - segsum(yv) * (1.0 / D)
    var = segsum(cen * cen) * (1.0 / D)
    y = cen * lax.rsqrt(var + RWKV_GN_EPS) * gg_ref[...] + gb_ref[...] + bonus
    o_ref[...] = (y * _silu(z_ref[...].astype(F32))).astype(BF16)


def _l1_rwkv(uf, uz, lo, p, batch, seq, width):
    L = RWKV_CHUNK
    QW = RWKV_QUAD_WIDTH
    wt = RWKV_QUADS_PER_STEP * QW
    ngrp = width // wt
    nst = seq // L
    vec = lambda: pl.BlockSpec((1, wt), lambda b, g, s: (0, g))
    col = lambda off: pl.BlockSpec((L, wt), lambda b, g, s: (b * nst + s, off + g))
    return pl.pallas_call(
        _rwkv_kernel,
        grid=(batch, ngrp, nst),
        in_specs=[col(0), col(ngrp), col(2 * ngrp), col(0),
                  pl.BlockSpec((L, 2 * LORA_PAD), lambda b, g, s: (b * nst + s, 0)),
                  vec(),
                  pl.BlockSpec((LORA_PAD, wt), lambda b, g, s: (0, g)),
                  vec(),
                  pl.BlockSpec((LORA_PAD, wt), lambda b, g, s: (0, g)),
                  vec(), vec(), vec(), vec(), vec()],
        out_specs=pl.BlockSpec((L, wt), lambda b, g, s: (b * nst + s, g)),
        out_shape=jax.ShapeDtypeStruct((batch * seq, width), BF16),
        scratch_shapes=[pltpu.VMEM((RWKV_QUADS_PER_STEP, QW, QW), F32)],
        compiler_params=_params(("arbitrary", "arbitrary", "arbitrary")),
        name="l1_rwkv",
    )(uf, uf, uf, uz, lo, p["w0"], p["w2"], p["a0"], p["a2"],
      p["kk"], p["ka"], p["rk"], p["gg"], p["gb"])


def _bd4_dense(w):
    nb, bs, _ = w.shape
    per = MXU_DIM // bs
    wb = w.reshape(nb // per, per, bs, bs)
    dense = jnp.einsum('cnij,nm->cnimj', wb, jnp.eye(per, dtype=w.dtype))
    return dense.reshape(nb // per, MXU_DIM, MXU_DIM).astype(BF16)


def _row(v):
    return v.reshape(1, -1).astype(F32)


def _pad_rows(w, rows):
    return jnp.pad(w, ((0, rows - w.shape[0]), (0, 0)))


def _pad_cols(w, cols):
    return jnp.pad(w, ((0, 0), (0, cols - w.shape[1])))


def kernel(x, l0_norm_pre, l0_w_in, l0_lru_conv_w, l0_lru_conv_b, l0_lru_wa, l0_lru_ba,
           l0_lru_wx, l0_lru_bx, l0_lru_lambda, l0_m_conv_w, l0_m_conv_b, l0_m_wq, l0_m_wk,
           l0_m_wv, l0_m_wi, l0_m_bi, l0_m_wf, l0_m_bf, l0_m_skip, l0_m_gn, l0_w_out,
           l0_norm_post, l1_norm_pre, l1_w_in, l1_mu_rkv, l1_mu_w, l1_mu_a, l1_w0, l1_w1,
           l1_w2, l1_a0, l1_a1, l1_a2, l1_k_k, l1_k_a, l1_r_k, l1_gn_g, l1_gn_b, l1_w_out,
           l1_norm_post):
    batch, seq, d = x.shape
    x2 = x.reshape(batch * seq, d).astype(F32)
    rw = l0_lru_lambda.shape[0]
    mw = l0_m_skip.shape[0]
    ww = l1_w0.shape[0]
    assert rw == mw, "column blocking assumes equal head-group widths"

    w_in0 = jnp.concatenate([l0_w_in[:, 0:rw], l0_w_in[:, 2 * rw:2 * rw + mw],
                             l0_w_in[:, rw:2 * rw], l0_w_in[:, 2 * rw + mw:]], axis=1).astype(BF16)
    uf0, uz0 = _l0_in(x2, _row(l0_norm_pre), w_in0, rw + mw)

    lru_p = dict(cw=l0_lru_conv_w.astype(F32), cb=_row(l0_lru_conv_b),
                 wa=l0_lru_wa.astype(BF16), ba=_row(l0_lru_ba),
                 wx=l0_lru_wx.astype(BF16), bx=_row(l0_lru_bx), lam=_row(l0_lru_lambda))
    yr = _l0_lru(uf0, uz0, lru_p, batch, seq, rw)

    wg = jnp.concatenate([l0_m_wi, l0_m_wf], axis=1)
    wg = _pad_cols(wg, LANES).reshape(3, mw, LANES).astype(BF16)
    bg = _pad_cols(jnp.concatenate([l0_m_bi, l0_m_bf]).reshape(1, -1), LANES).astype(F32)
    m_p = dict(cw=l0_m_conv_w.astype(F32), cb=_row(l0_m_conv_b),
               wq=_bd4_dense(l0_m_wq), wk=_bd4_dense(l0_m_wk),
               wv=_bd4_dense(l0_m_wv), wg=wg, bg=bg,
               skip=_row(l0_m_skip), gn=_row(l0_m_gn))
    ym = _l0_mlstm(uf0, uz0, m_p, batch, seq, mw, x_col=rw // mw, z_col=rw // mw)

    w_out0 = l0_w_out.astype(BF16).reshape(2, rw, d)
    x1 = _out_proj([yr, ym], w_out0, _row(l0_norm_post), x2)

    w1 = _pad_cols(l1_w1, LORA_PAD).astype(BF16)
    a1 = _pad_cols(l1_a1, LORA_PAD).astype(BF16)
    uf1, uz1, lo1 = _l1_in(x1, _row(l1_norm_pre), l1_w_in.astype(BF16), _row(l1_mu_w),
                           _row(l1_mu_a), w1, a1, _row(l1_mu_rkv), seq)
    r_p = dict(w0=_row(l1_w0),
               w2=_pad_rows(l1_w2, LORA_PAD).astype(BF16), a0=_row(l1_a0),
               a2=_pad_rows(l1_a2, LORA_PAD).astype(BF16), kk=_row(l1_k_k), ka=_row(l1_k_a),
               rk=_row(l1_r_k), gg=_row(l1_gn_g), gb=_row(l1_gn_b))
    y1 = _l1_rwkv(uf1, uz1, lo1, r_p, batch, seq, ww)
    out = _out_proj([y1], l1_w_out.astype(BF16).reshape(1, ww, d), _row(l1_norm_post), x1)
    return out.reshape(batch, seq, d).astype(x.dtype)
```

```python
import functools

import jax
import jax.numpy as jnp
from jax import lax
from jax.experimental import pallas as pl
from jax.experimental.pallas import tpu as pltpu

F32 = jnp.float32
BF16 = jnp.bfloat16

RMS_EPS = 1e-6
CONV_WIDTH = 4
LRU_C = 8.0
LRU_BLOCK = 128
MLSTM_HEADS = 8
MLSTM_QKV_BLOCK = 4
MLSTM_CHUNK = 128
MLSTM_GN_EPS = 1e-6
RWKV_HEAD_DIM = 64
RWKV_GN_EPS = 64e-5
RWKV_CHUNK = 64
RWKV_DECAY_SCALE = 0.6065306597126334
RWKV_QUAD_WIDTH = 256
RWKV_QUADS_PER_STEP = 16
LORA_PAD = 128

LANES = 128
SUBLANES = 8
MXU_DIM = 256
VMEM_LIMIT = 56 * 1024 * 1024


def _params(sem):
    return pltpu.CompilerParams(dimension_semantics=sem, vmem_limit_bytes=VMEM_LIMIT)


def _sigmoid(x):
    return 1.0 / (1.0 + jnp.exp(-x))


def _silu(x):
    return x * _sigmoid(x)


def _softplus(x):
    return jnp.maximum(x, 0.0) + jnp.log1p(jnp.exp(-jnp.abs(x)))


def _dot(a, b):
    return jnp.dot(a.astype(BF16), b.astype(BF16), preferred_element_type=F32)


def _dot_nt(a, b):
    return lax.dot_general(a.astype(BF16), b.astype(BF16), (((1,), (1,)), ((), ())),
                           preferred_element_type=F32)


def _dot_tn(a, b):
    return lax.dot_general(a.astype(BF16), b.astype(BF16), (((0,), (0,)), ((), ())),
                           preferred_element_type=F32)


def _split3(x):
    hi = x.astype(BF16)
    r1 = x - hi.astype(F32)
    mid = r1.astype(BF16)
    lo = (r1 - mid.astype(F32)).astype(BF16)
    return hi, mid, lo


def _dot_exact_rhs(a_f32, b_bf16):
    hi, mid, lo = _split3(a_f32)
    out = jnp.dot(hi, b_bf16, preferred_element_type=F32)
    out = out + jnp.dot(mid, b_bf16, preferred_element_type=F32)
    return out + jnp.dot(lo, b_bf16, preferred_element_type=F32)


def _dot_exact_lhs(a_bf16, b_f32):
    hi, mid, lo = _split3(b_f32)
    out = jnp.dot(a_bf16, hi, preferred_element_type=F32)
    out = out + jnp.dot(a_bf16, mid, preferred_element_type=F32)
    return out + jnp.dot(a_bf16, lo, preferred_element_type=F32)


def _rms_norm(x, g):
    return x * lax.rsqrt(jnp.mean(x * x, axis=-1, keepdims=True) + RMS_EPS) * g


def _tri(n, inclusive=True):
    r = lax.broadcasted_iota(jnp.int32, (n, n), 0)
    c = lax.broadcasted_iota(jnp.int32, (n, n), 1)
    return (c <= r) if inclusive else (c < r)


def _l0_in_kernel(x_ref, g_ref, w_ref, of_ref, oz_ref, h_ref, *, nf):
    j = pl.program_id(1)

    @pl.when(j == 0)
    def _():
        h_ref[...] = _rms_norm(x_ref[...], g_ref[...]).astype(BF16)

    acc = jnp.dot(h_ref[...], w_ref[...], preferred_element_type=F32)

    @pl.when(j < nf)
    def _():
        of_ref[...] = acc

    @pl.when(j >= nf)
    def _():
        oz_ref[...] = acc.astype(BF16)


def _l0_in(x2, g, w_bf16, n_f32, tm=1024, tn=1024):
    t, d = x2.shape
    n = w_bf16.shape[1]
    nf = n_f32 // tn
    nz = (n - n_f32) // tn
    return pl.pallas_call(
        functools.partial(_l0_in_kernel, nf=nf),
        grid=(t // tm, nf + nz),
        in_specs=[pl.BlockSpec((tm, d), lambda i, j: (i, 0)),
                  pl.BlockSpec((1, d), lambda i, j: (0, 0)),
                  pl.BlockSpec((d, tn), lambda i, j: (0, j))],
        out_specs=[pl.BlockSpec((tm, tn), lambda i, j: (i, jnp.minimum(j, nf - 1))),
                   pl.BlockSpec((tm, tn), lambda i, j: (i, jnp.maximum(j - nf, 0)))],
        out_shape=[jax.ShapeDtypeStruct((t, n_f32), F32),
                   jax.ShapeDtypeStruct((t, n - n_f32), BF16)],
        scratch_shapes=[pltpu.VMEM((tm, d), BF16)],
        compiler_params=_params(("arbitrary", "arbitrary")),
        name="l0_in",
    )(x2, g, w_bf16)


def _causal_conv(x, tail_ref, w, b):
    ts = x.shape[0]
    row = lax.broadcasted_iota(jnp.int32, (SUBLANES, x.shape[1]), 0)
    tail = tail_ref[...]
    y = b + x * w[CONV_WIDTH - 1:CONV_WIDTH, :]
    for k in range(1, CONV_WIDTH):
        rolled = pltpu.roll(x, k, 0)
        top = jnp.where(row < k, pltpu.roll(tail, k, 0), rolled[0:SUBLANES])
        xk = jnp.concatenate([top, rolled[SUBLANES:]], axis=0)
        y = y + xk * w[CONV_WIDTH - 1 - k:CONV_WIDTH - k, :]
    tail_ref[...] = x[ts - SUBLANES:ts]
    return y


def _lru_kernel(x_ref, z_ref, cw_ref, cb_ref, wa_ref, ba_ref, wx_ref, bx_ref, lam_ref, o_ref,
                xbuf, a_p, u_p, hl_s, al_s, hc_s, *, ts, nblk, pitch):
    s = pl.program_id(2)

    @pl.when(s == 0)
    def _():
        xbuf[...] = jnp.zeros_like(xbuf)
        hc_s[...] = jnp.zeros_like(hc_s)

    xc = _causal_conv(x_ref[...], xbuf, cw_ref[...], cb_ref[...])

    gr, gi = [], []
    for n in range(nblk):
        xn = xc[:, n * LRU_BLOCK:(n + 1) * LRU_BLOCK]
        gr.append(_dot(xn, wa_ref[n]))
        gi.append(_dot(xn, wx_ref[n]))
    r = _sigmoid(jnp.concatenate(gr, axis=1) + ba_ref[...])
    i = _sigmoid(jnp.concatenate(gi, axis=1) + bx_ref[...])
    log_a = -LRU_C * r * _softplus(-lam_ref[...])
    a = jnp.exp(log_a)
    u = jnp.sqrt(-jnp.tanh(log_a) * (a * a + 1.0)) * (i * xc)

    seg = ts // SUBLANES
    nslab = a.shape[1] // LANES
    for c in range(nslab):
        for j in range(SUBLANES):
            a_p[c, pl.ds(j * pitch, seg), :] = a[j * seg:(j + 1) * seg, c * LANES:(c + 1) * LANES]
            u_p[c, pl.ds(j * pitch, seg), :] = u[j * seg:(j + 1) * seg, c * LANES:(c + 1) * LANES]

    def local_scan(t, carry):
        out = []
        for c in range(nslab):
            h, acc = carry[c]
            at = a_p[c, pl.ds(t, SUBLANES, stride=pitch), :]
            ut = u_p[c, pl.ds(t, SUBLANES, stride=pitch), :]
            h = at * h + ut
            acc = at * acc
            hl_s[c, t] = h
            al_s[c, t] = acc
            out.append((h, acc))
        return tuple(out)

    init = tuple((jnp.zeros((SUBLANES, LANES), F32), jnp.ones((SUBLANES, LANES), F32))
                 for _ in range(nslab))
    ends = lax.fori_loop(0, seg, local_scan, init, unroll=4)

    row = lax.broadcasted_iota(jnp.int32, (SUBLANES, LANES), 0)
    for c in range(nslab):
        hs, ac = ends[c]
        for sft in (1, 2, 4):
            a_sh = jnp.where(row >= sft, pltpu.roll(ac, sft, 0), 1.0)
            h_sh = jnp.where(row >= sft, pltpu.roll(hs, sft, 0), 0.0)
            hs = ac * h_sh + hs
            ac = ac * a_sh
        carry = hc_s[:, c * LANES:(c + 1) * LANES]
        h_end = ac * carry + hs
        h_in = jnp.where(row == 0, carry, pltpu.roll(h_end, 1, 0))
        hc_s[:, c * LANES:(c + 1) * LANES] = h_end[SUBLANES - 1:SUBLANES, :]

        def fix(t, _, c=c, h_in=h_in):
            u_p[c, pl.ds(t, SUBLANES, stride=pitch), :] = hl_s[c, t] + al_s[c, t] * h_in
            return 0

        lax.fori_loop(0, seg, fix, 0, unroll=4)

    for c in range(nslab):
        for j in range(SUBLANES):
            rows = slice(j * seg, (j + 1) * seg)
            lanes = slice(c * LANES, (c + 1) * LANES)
            hj = u_p[c, pl.ds(j * pitch, seg), :]
            o_ref[rows, lanes] = (hj * _silu(z_ref[rows, lanes].astype(F32))).astype(BF16)


def _l0_lru(uf, uz, p, batch, seq, width, ts=512, tc=512):
    nblk = tc // LRU_BLOCK
    nct = width // tc
    nst = seq // ts
    seg = ts // SUBLANES
    pitch = seg + SUBLANES
    assert seg % (2 * SUBLANES) == 0
    nslab = tc // LANES
    vec = lambda: pl.BlockSpec((1, tc), lambda b, c, s: (0, c))
    return pl.pallas_call(
        functools.partial(_lru_kernel, ts=ts, nblk=nblk, pitch=pitch),
        grid=(batch, nct, nst),
        in_specs=[pl.BlockSpec((ts, tc), lambda b, c, s: (b * nst + s, c)),
                  pl.BlockSpec((ts, tc), lambda b, c, s: (b * nst + s, c)),
                  pl.BlockSpec((CONV_WIDTH, tc), lambda b, c, s: (0, c)),
                  vec(),
                  pl.BlockSpec((nblk, LRU_BLOCK, LRU_BLOCK), lambda b, c, s: (c, 0, 0)),
                  vec(),
                  pl.BlockSpec((nblk, LRU_BLOCK, LRU_BLOCK), lambda b, c, s: (c, 0, 0)),
                  vec(), vec()],
        out_specs=pl.BlockSpec((ts, tc), lambda b, c, s: (b * nst + s, c)),
        out_shape=jax.ShapeDtypeStruct((batch * seq, width), BF16),
        scratch_shapes=[pltpu.VMEM((SUBLANES, tc), F32),
                        pltpu.VMEM((nslab, SUBLANES * pitch, LANES), F32),
                        pltpu.VMEM((nslab, SUBLANES * pitch, LANES), F32),
                        pltpu.VMEM((nslab, seg, SUBLANES, LANES), F32),
                        pltpu.VMEM((nslab, seg, SUBLANES, LANES), F32),
                        pltpu.VMEM((1, tc), F32)],
        compiler_params=_params(("arbitrary", "arbitrary", "arbitrary")),
        name="l0_lru",
    )(uf, uz, p["cw"], p["cb"], p["wa"], p["ba"], p["wx"], p["bx"], p["lam"])


def _bd4(xb, w_ref):
    nb, bw, _ = w_ref.shape
    return jnp.concatenate(
        [jnp.dot(xb[:, c * bw:(c + 1) * bw], w_ref[c], preferred_element_type=F32)
         for c in range(nb)], axis=1)


def _mlstm_kernel(x_ref, z_ref, cw_ref, cb_ref, wq_ref, wk_ref, wv_ref, wg_ref, bg_ref,
                  skip_ref, gn_ref, o_ref, xbuf, c_s, n_s, m_s, *, hd):
    L = MLSTM_CHUNK
    H = MLSTM_HEADS
    s = pl.program_id(1)

    @pl.when(s == 0)
    def _():
        xbuf[...] = jnp.zeros_like(xbuf)
        c_s[...] = jnp.zeros_like(c_s)
        n_s[...] = jnp.zeros_like(n_s)
        m_s[...] = jnp.zeros_like(m_s)

    x = x_ref[...]
    xmc = _silu(_causal_conv(x, xbuf, cw_ref[...], cb_ref[...]))

    xmc_b = xmc.astype(BF16)
    q = _bd4(xmc_b, wq_ref)
    k = _bd4(xmc_b, wk_ref)
    v = _bd4(x.astype(BF16), wv_ref)

    g = _dot(q, wg_ref[0]) + _dot(k, wg_ref[1]) + _dot(v, wg_ref[2]) + bg_ref[...]
    lane = lax.broadcasted_iota(jnp.int32, (L, LANES), 1)
    lf = jnp.minimum(g, 0.0) - jnp.log1p(jnp.exp(-jnp.abs(g)))
    lf = jnp.where((lane >= H) & (lane < 2 * H), lf, 0.0)
    csum = _dot_exact_lhs(_tri(L).astype(BF16), lf)
    gb = jnp.where(lane < H, g, csum)
    gbt = gb.T

    causal = _tri(L)
    kscale = hd ** -0.5
    heads = range(H)
    sls = [slice(h * hd, (h + 1) * hd) for h in heads]
    qb = [q[:, sl].astype(BF16) for sl in sls]
    kh = [k[:, sl] * kscale for sl in sls]
    vb = [v[:, sl].astype(BF16) for sl in sls]
    s_raw = [lax.dot_general(qb[h], kh[h].astype(BF16), (((1,), (1,)), ((), ())),
                             preferred_element_type=F32) for h in heads]
    qc = [jnp.dot(qb[h], c_s[h].astype(BF16), preferred_element_type=F32) for h in heads]

    wgt, inter, m_t, decay, kw, m_new = [], [], [], [], [], []
    for h in heads:
        b_col = gb[:, H + h:H + h + 1]
        i_col = gb[:, h:h + 1]
        b_row = gbt[H + h:H + h + 1, :]
        i_row = gbt[h:h + 1, :]
        m_prev = m_s[h][0:1, 0:1]
        dm = jnp.where(causal, b_col - b_row + i_row, -1e30)
        m_inter = b_col + m_prev
        mt = jnp.maximum(jnp.max(dm, axis=-1, keepdims=True), m_inter)
        wgt.append(jnp.exp(dm - mt))
        inter.append(jnp.exp(m_inter - mt))
        m_t.append(mt)
        b_last = b_col[L - 1:L, :]
        g_row = b_last - b_row + i_row
        g_col = b_last - b_col + i_col
        mn = jnp.maximum(b_last + m_prev, jnp.max(g_row, axis=-1, keepdims=True))
        m_new.append(mn)
        decay.append(jnp.exp(b_last + m_prev - mn))
        kw.append(kh[h] * jnp.exp(g_col - mn))

    upd = [lax.dot_general(kw[h].astype(BF16), vb[h], (((0,), (0,)), ((), ())),
                           preferred_element_type=F32) for h in heads]
    scores = [s_raw[h] * wgt[h] for h in heads]
    num = [jnp.dot(scores[h].astype(BF16), vb[h], preferred_element_type=F32)
           + inter[h] * qc[h] for h in heads]

    ys = []
    for h in heads:
        nvec = n_s[h]
        den = (jnp.sum(scores[h], axis=-1, keepdims=True)
               + inter[h] * jnp.sum(q[:, sls[h]] * nvec, axis=-1, keepdims=True))
        hout = num[h] / jnp.maximum(jnp.abs(den), jnp.exp(-m_t[h]))
        c_s[h] = decay[h] * c_s[h] + upd[h]
        n_s[h] = decay[h] * nvec + jnp.sum(kw[h], axis=0, keepdims=True)
        m_s[h] = jnp.broadcast_to(m_new[h], (SUBLANES, LANES))
        mu = jnp.mean(hout, axis=-1, keepdims=True)
        cen = hout - mu
        var = jnp.mean(cen * cen, axis=-1, keepdims=True)
        ys.append(cen * lax.rsqrt(var + MLSTM_GN_EPS))
    y = jnp.concatenate(ys, axis=1) * gn_ref[...] + skip_ref[...] * xmc
    o_ref[...] = (y * _silu(z_ref[...].astype(F32))).astype(BF16)


def _l0_mlstm(uf, uz, p, batch, seq, width, x_col, z_col):
    L = MLSTM_CHUNK
    nst = seq // L
    hd = width // MLSTM_HEADS
    full = lambda shape: pl.BlockSpec(shape, lambda b, s: (0,) * len(shape))
    return pl.pallas_call(
        functools.partial(_mlstm_kernel, hd=hd),
        grid=(batch, nst),
        in_specs=[pl.BlockSpec((L, width), lambda b, s: (b * nst + s, x_col)),
                  pl.BlockSpec((L, width), lambda b, s: (b * nst + s, z_col)),
                  full((CONV_WIDTH, width)), full((1, width)),
                  full((width // MXU_DIM, MXU_DIM, MXU_DIM)),
                  full((width // MXU_DIM, MXU_DIM, MXU_DIM)),
                  full((width // MXU_DIM, MXU_DIM, MXU_DIM)),
                  full((3, width, LANES)), full((1, LANES)),
                  full((1, width)), full((1, width))],
        out_specs=pl.BlockSpec((L, width), lambda b, s: (b * nst + s, 0)),
        out_shape=jax.ShapeDtypeStruct((batch * seq, width), BF16),
        scratch_shapes=[pltpu.VMEM((SUBLANES, width), F32),
                        pltpu.VMEM((MLSTM_HEADS, hd, hd), F32),
                        pltpu.VMEM((MLSTM_HEADS, 1, hd), F32),
                        pltpu.VMEM((MLSTM_HEADS, SUBLANES, LANES), F32)],
        compiler_params=_params(("arbitrary", "arbitrary")),
        name="l0_mlstm",
    )(uf, uz, p["cw"], p["cb"], p["wq"], p["wk"], p["wv"], p["wg"], p["bg"], p["skip"], p["gn"])


def _out_kernel(*refs, nin):
    y_refs = refs[:nin]
    w_ref, g_ref, x_ref, o_ref = refs[nin:]
    acc = jnp.dot(y_refs[0][...], w_ref[0], preferred_element_type=F32)
    for n in range(1, nin):
        acc = acc + jnp.dot(y_refs[n][...], w_ref[n], preferred_element_type=F32)
    o_ref[...] = x_ref[...] + _rms_norm(acc, g_ref[...])


def _out_proj(ys, w_bf16, g, x2, tm=256):
    nin, kin, d = w_bf16.shape
    t = x2.shape[0]
    return pl.pallas_call(
        functools.partial(_out_kernel, nin=nin),
        grid=(t // tm,),
        in_specs=[pl.BlockSpec((tm, kin), lambda i: (i, 0)) for _ in range(nin)]
        + [pl.BlockSpec((nin, kin, d), lambda i: (0, 0, 0), pipeline_mode=pl.Buffered(1)),
           pl.BlockSpec((1, d), lambda i: (0, 0)),
           pl.BlockSpec((tm, d), lambda i: (i, 0))],
        out_specs=pl.BlockSpec((tm, d), lambda i: (i, 0)),
        out_shape=jax.ShapeDtypeStruct((t, d), F32),
        compiler_params=_params(("arbitrary",)),
        name="out_proj",
    )(*ys, w_bf16, g, x2)


def _shift_rows(x, last_ref, first):
    row = lax.broadcasted_iota(jnp.int32, x.shape, 0)
    last = jnp.where(first, 0.0, last_ref[...])
    prev = jnp.where(row == 0, last, pltpu.roll(x, 1, 0))
    last_ref[...] = x[x.shape[0] - 1:x.shape[0], :]
    return prev


def _l1_in_kernel(x_ref, g_ref, w_ref, muw_ref, mua_ref, w1_ref, a1_ref, mu_ref, of_ref, oz_ref,
                  lo_ref, h_ref, last_ref, lastu_ref, *, nf, tiles_per_seq):
    i = pl.program_id(0)
    j = pl.program_id(1)
    first = i % tiles_per_seq == 0

    @pl.when(j == 0)
    def _():
        h = _rms_norm(x_ref[...], g_ref[...])
        h_ref[...] = h.astype(BF16)
        dh = _shift_rows(h, last_ref, first) - h
        xw = h + dh * muw_ref[...]
        xa = h + dh * mua_ref[...]
        lo_ref[:, 0:LORA_PAD] = jnp.tanh(_dot(xw, w1_ref[...]))
        lo_ref[:, LORA_PAD:2 * LORA_PAD] = _dot(xa, a1_ref[...])

    acc = jnp.dot(h_ref[...], w_ref[...], preferred_element_type=F32)
    mixed = acc + (_shift_rows(acc, lastu_ref.at[j], first) - acc) * mu_ref[...]

    @pl.when(j < nf)
    def _():
        of_ref[...] = mixed

    @pl.when(j >= nf)
    def _():
        oz_ref[...] = mixed.astype(BF16)


def _l1_in(x2, g, w_bf16, muw, mua, w1, a1, mu_rkv, seq, tm=1024, tn=1024):
    t, d = x2.shape
    n = w_bf16.shape[1]
    n_f32 = mu_rkv.shape[1]
    nf = n_f32 // tn
    nz = (n - n_f32) // tn
    mu_rkv = _pad_cols(mu_rkv, n)
    c2 = lambda shape: pl.BlockSpec(shape, lambda i, j: (0, 0))
    return pl.pallas_call(
        functools.partial(_l1_in_kernel, nf=nf, tiles_per_seq=seq // tm),
        grid=(t // tm, nf + nz),
        in_specs=[pl.BlockSpec((tm, d), lambda i, j: (i, 0)),
                  c2((1, d)),
                  pl.BlockSpec((d, tn), lambda i, j: (0, j)),
                  c2((1, d)), c2((1, d)), c2((d, LORA_PAD)), c2((d, LORA_PAD)),
                  pl.BlockSpec((1, tn), lambda i, j: (0, j))],
        out_specs=[pl.BlockSpec((tm, tn), lambda i, j: (i, jnp.minimum(j, nf - 1))),
                   pl.BlockSpec((tm, tn), lambda i, j: (i, jnp.maximum(j - nf, 0))),
                   pl.BlockSpec((tm, 2 * LORA_PAD), lambda i, j: (i, 0))],
        out_shape=[jax.ShapeDtypeStruct((t, n_f32), F32),
                   jax.ShapeDtypeStruct((t, n - n_f32), BF16),
                   jax.ShapeDtypeStruct((t, 2 * LORA_PAD), F32)],
        scratch_shapes=[pltpu.VMEM((tm, d), BF16), pltpu.VMEM((1, d), F32),
                        pltpu.VMEM((nf + nz, 1, tn), F32)],
        compiler_params=_params(("arbitrary", "arbitrary")),
        name="l1_in",
    )(x2, g, w_bf16, muw, mua, w1, a1, mu_rkv)


def _bdot(a, b):
    return jnp.dot(a, b, preferred_element_type=F32)


def _rwkv_kernel(r_ref, k_ref, v_ref, z_ref, lo_ref, w0_ref, w2_ref,
                 a0_ref, a2_ref, kk_ref, ka_ref, rk_ref, gg_ref, gb_ref, o_ref, st_s):
    L = RWKV_CHUNK
    D = RWKV_HEAD_DIM
    QW = RWKV_QUAD_WIDTH
    HQ = QW // D
    NQ = r_ref.shape[1] // QW
    s = pl.program_id(2)

    @pl.when(s == 0)
    def _():
        st_s[...] = jnp.zeros_like(st_s)

    r = r_ref[...]
    k = k_ref[...]
    v = v_ref[...]

    lo = lo_ref[...]
    logw = -RWKV_DECAY_SCALE * _sigmoid(w0_ref[...] + _dot(lo[:, 0:LORA_PAD], w2_ref[...]))
    a = _sigmoid(a0_ref[...] + _dot(lo[:, LORA_PAD:2 * LORA_PAD], a2_ref[...]))

    seg = ((lax.broadcasted_iota(jnp.int32, (QW, QW), 0) // D)
           == (lax.broadcasted_iota(jnp.int32, (QW, QW), 1) // D))
    seg_b = seg.astype(BF16)
    qs = [slice(q * QW, (q + 1) * QW) for q in range(NQ)]

    def segsum(x):
        stacked = jnp.concatenate([x[:, sl] for sl in qs], axis=0).astype(BF16)
        res = _bdot(stacked, seg_b)
        return jnp.concatenate([res[q * L:(q + 1) * L] for q in range(NQ)], axis=1)

    kk = k * kk_ref[...]
    kk = kk * lax.rsqrt(jnp.maximum(segsum(kk * kk), 1e-24))
    k = k * (1.0 + (a - 1.0) * ka_ref[...])
    bonus = segsum(r * k * rk_ref[...]) * v

    tri_b = _tri(L).astype(BF16)
    lw_hi = logw.astype(BF16)
    lw_lo = (logw - lw_hi.astype(F32)).astype(BF16)
    cs = _bdot(tri_b, lw_hi) + _bdot(tri_b, lw_lo)
    p_in = jnp.exp(cs)
    p_inv = jnp.exp(-cs)
    p_ex = jnp.exp(cs - logw)
    p_last = p_in[L - 1:L, :]

    rt = r * p_in
    at = -kk * p_ex
    kt = k * p_inv
    bt = kk * a * p_inv

    per_tile = LANES // D
    lane = lax.broadcasted_iota(jnp.int32, (1, LANES), 1) // D
    hmb = [(lane == h).astype(BF16) for h in range(per_tile)]
    rowi = lax.broadcasted_iota(jnp.int32, (L, QW), 0)
    coli = lax.broadcasted_iota(jnp.int32, (L, QW), 1) % D
    strict = coli < rowi
    lower = coli <= rowi
    eye = (coli == rowi).astype(F32)
    zero_tile = jnp.zeros((L, LANES), BF16)

    def head_rows(mb):
        blocks = []
        for h in range(HQ):
            t = h // per_tile
            keep = mb[:, t * LANES:(t + 1) * LANES] * hmb[h % per_tile]
            blocks.append(jnp.concatenate(
                [keep if c == t else zero_tile for c in range(QW // LANES)], axis=1))
        return blocks

    def bd(m):
        return jnp.concatenate(head_rows(m.astype(BF16)), axis=0)

    def cat2(x, y):
        return jnp.concatenate([x, y], axis=0)

    xs, zs, vs, a_ab, a_ak, a_rb, a_rk = [], [], [], [], [], [], []
    for sl in qs:
        xq = cat2(at[:, sl], rt[:, sl]).astype(BF16)
        btb = bt[:, sl].astype(BF16)
        ktb = kt[:, sl].astype(BF16)
        ycat = jnp.concatenate(head_rows(btb) + head_rows(ktb), axis=0)
        aall = lax.dot_general(xq, ycat, (((1,), (1,)), ((), ())),
                               preferred_element_type=F32)
        xs.append(xq)
        zs.append(cat2(btb, ktb))
        vs.append(v[:, sl])
        a_ab.append(jnp.where(strict, aall[0:L, 0:QW], 0.0))
        a_ak.append(jnp.where(strict, aall[0:L, QW:2 * QW], 0.0))
        a_rb.append(jnp.where(lower, aall[L:2 * L, 0:QW], 0.0))
        a_rk.append(jnp.where(lower, aall[L:2 * L, QW:2 * QW], 0.0))

    def bdb(mb):
        return jnp.concatenate(head_rows(mb), axis=0)

    tinv = [eye + m for m in a_ab]
    pwb = [m.astype(BF16) for m in a_ab]
    pwb = [_bdot(pwb[q], bdb(pwb[q])).astype(BF16) for q in range(NQ)]
    sx = [_bdot(xs[q], st_s[q].astype(BF16)) for q in range(NQ)]
    av = [_bdot(cat2(a_ak[q], a_rk[q]).astype(BF16), bd(vs[q])) for q in range(NQ)]
    for _ in range(L.bit_length() - 3):
        both = [_bdot(cat2(tinv[q].astype(BF16), pwb[q]), bdb(pwb[q])) for q in range(NQ)]
        tinv = [tinv[q] + both[q][0:L] for q in range(NQ)]
        pwb = [both[q][L:2 * L].astype(BF16) for q in range(NQ)]
    tinv = [tinv[q] + _bdot(tinv[q].astype(BF16), bdb(pwb[q])) for q in range(NQ)]

    us = [_bdot(tinv[q].astype(BF16), bd(sx[q][0:L] + av[q][0:L])) for q in range(NQ)]
    ys = [sx[q][L:2 * L] + av[q][L:2 * L] + _bdot(a_rb[q].astype(BF16), bd(us[q]))
          for q in range(NQ)]

    for q, sl in enumerate(qs):
        upd = lax.dot_general(zs[q], cat2(us[q], vs[q]).astype(BF16),
                              (((0,), (0,)), ((), ())),
                              preferred_element_type=F32)
        cols = []
        for cb in range(QW // LANES):
            lo_l = q * QW + cb * LANES
            cols.append(jnp.broadcast_to(p_last[:, lo_l:lo_l + LANES], (LANES, LANES)).T)
        pcol = jnp.concatenate(cols, axis=0)
        pcol = jnp.concatenate([pcol] * (QW // LANES), axis=1)
        st_s[q] = pcol * (st_s[q] + jnp.where(seg, upd, 0.0))

    yv = jnp.concatenate(ys, axis=1)
    cen = yv - segsum(yv) * (1.0 / D)
    var = segsum(cen * cen) * (1.0 / D)
    y = cen * lax.rsqrt(var + RWKV_GN_EPS) * gg_ref[...] + gb_ref[...] + bonus
    o_ref[...] = (y * _silu(z_ref[...].astype(F32))).astype(BF16)


def _l1_rwkv(uf, uz, lo, p, batch, seq, width):
    L = RWKV_CHUNK
    QW = RWKV_QUAD_WIDTH
    wt = RWKV_QUADS_PER_STEP * QW
    ngrp = width // wt
    nst = seq // L
    vec = lambda: pl.BlockSpec((1, wt), lambda b, g, s: (0, g))
    col = lambda off: pl.BlockSpec((L, wt), lambda b, g, s: (b * nst + s, off + g))
    return pl.pallas_call(
        _rwkv_kernel,
        grid=(batch, ngrp, nst),
        in_specs=[col(0), col(ngrp), col(2 * ngrp), col(0),
                  pl.BlockSpec((L, 2 * LORA_PAD), lambda b, g, s: (b * nst + s, 0)),
                  vec(),
                  pl.BlockSpec((LORA_PAD, wt), lambda b, g, s: (0, g)),
                  vec(),
                  pl.BlockSpec((LORA_PAD, wt), lambda b, g, s: (0, g)),
                  vec(), vec(), vec(), vec(), vec()],
        out_specs=pl.BlockSpec((L, wt), lambda b, g, s: (b * nst + s, g)),
        out_shape=jax.ShapeDtypeStruct((batch * seq, width), BF16),
        scratch_shapes=[pltpu.VMEM((RWKV_QUADS_PER_STEP, QW, QW), F32)],
        compiler_params=_params(("arbitrary", "arbitrary", "arbitrary")),
        name="l1_rwkv",
    )(uf, uf, uf, uz, lo, p["w0"], p["w2"], p["a0"], p["a2"],
      p["kk"], p["ka"], p["rk"], p["gg"], p["gb"])


def _bd4_dense(w):
    nb, bs, _ = w.shape
    per = MXU_DIM // bs
    wb = w.reshape(nb // per, per, bs, bs)
    dense = jnp.einsum('cnij,nm->cnimj', wb, jnp.eye(per, dtype=w.dtype))
    return dense.reshape(nb // per, MXU_DIM, MXU_DIM).astype(BF16)


def _row(v):
    return v.reshape(1, -1).astype(F32)


def _pad_rows(w, rows):
    return jnp.pad(w, ((0, rows - w.shape[0]), (0, 0)))


def _pad_cols(w, cols):
    return jnp.pad(w, ((0, 0), (0, cols - w.shape[1])))


def kernel(x, l0_norm_pre, l0_w_in, l0_lru_conv_w, l0_lru_conv_b, l0_lru_wa, l0_lru_ba,
           l0_lru_wx, l0_lru_bx, l0_lru_lambda, l0_m_conv_w, l0_m_conv_b, l0_m_wq, l0_m_wk,
           l0_m_wv, l0_m_wi, l0_m_bi, l0_m_wf, l0_m_bf, l0_m_skip, l0_m_gn, l0_w_out,
           l0_norm_post, l1_norm_pre, l1_w_in, l1_mu_rkv, l1_mu_w, l1_mu_a, l1_w0, l1_w1,
           l1_w2, l1_a0, l1_a1, l1_a2, l1_k_k, l1_k_a, l1_r_k, l1_gn_g, l1_gn_b, l1_w_out,
           l1_norm_post):
    batch, seq, d = x.shape
    x2 = x.reshape(batch * seq, d).astype(F32)
    rw = l0_lru_lambda.shape[0]
    mw = l0_m_skip.shape[0]
    ww = l1_w0.shape[0]
    assert rw == mw, "column blocking assumes equal head-group widths"

    w_in0 = jnp.concatenate([l0_w_in[:, 0:rw], l0_w_in[:, 2 * rw:2 * rw + mw],
                             l0_w_in[:, rw:2 * rw], l0_w_in[:, 2 * rw + mw:]], axis=1).astype(BF16)
    uf0, uz0 = _l0_in(x2, _row(l0_norm_pre), w_in0, rw + mw)

    lru_p = dict(cw=l0_lru_conv_w.astype(F32), cb=_row(l0_lru_conv_b),
                 wa=l0_lru_wa.astype(BF16), ba=_row(l0_lru_ba),
                 wx=l0_lru_wx.astype(BF16), bx=_row(l0_lru_bx), lam=_row(l0_lru_lambda))
    yr = _l0_lru(uf0, uz0, lru_p, batch, seq, rw)

    wg = jnp.concatenate([l0_m_wi, l0_m_wf], axis=1)
    wg = _pad_cols(wg, LANES).reshape(3, mw, LANES).astype(BF16)
    bg = _pad_cols(jnp.concatenate([l0_m_bi, l0_m_bf]).reshape(1, -1), LANES).astype(F32)
    m_p = dict(cw=l0_m_conv_w.astype(F32), cb=_row(l0_m_conv_b),
               wq=_bd4_dense(l0_m_wq), wk=_bd4_dense(l0_m_wk),
               wv=_bd4_dense(l0_m_wv), wg=wg, bg=bg,
               skip=_row(l0_m_skip), gn=_row(l0_m_gn))
    ym = _l0_mlstm(uf0, uz0, m_p, batch, seq, mw, x_col=rw // mw, z_col=rw // mw)

    w_out0 = l0_w_out.astype(BF16).reshape(2, rw, d)
    x1 = _out_proj([yr, ym], w_out0, _row(l0_norm_post), x2)

    w1 = _pad_cols(l1_w1, LORA_PAD).astype(BF16)
    a1 = _pad_cols(l1_a1, LORA_PAD).astype(BF16)
    uf1, uz1, lo1 = _l1_in(x1, _row(l1_norm_pre), l1_w_in.astype(BF16), _row(l1_mu_w),
                           _row(l1_mu_a), w1, a1, _row(l1_mu_rkv), seq)
    r_p = dict(w0=_row(l1_w0),
               w2=_pad_rows(l1_w2, LORA_PAD).astype(BF16), a0=_row(l1_a0),
               a2=_pad_rows(l1_a2, LORA_PAD).astype(BF16), kk=_row(l1_k_k), ka=_row(l1_k_a),
               rk=_row(l1_r_k), gg=_row(l1_gn_g), gb=_row(l1_gn_b))
    y1 = _l1_rwkv(uf1, uz1, lo1, r_p, batch, seq, ww)
    out = _out_proj([y1], l1_w_out.astype(BF16).reshape(1, ww, d), _row(l1_norm_post), x1)
    return out.reshape(batch, seq, d).astype(x.dtype)
```

```python
import functools

import jax
import jax.numpy as jnp
from jax import lax
from jax.experimental import pallas as pl
from jax.experimental.pallas import tpu as pltpu

F32 = jnp.float32
BF16 = jnp.bfloat16

RMS_EPS = 1e-6
CONV_WIDTH = 4
LRU_C = 8.0
LRU_BLOCK = 128
MLSTM_HEADS = 8
MLSTM_QKV_BLOCK = 4
MLSTM_CHUNK = 128
MLSTM_GN_EPS = 1e-6
RWKV_HEAD_DIM = 64
RWKV_GN_EPS = 64e-5
RWKV_CHUNK = 64
RWKV_DECAY_SCALE = 0.6065306597126334
RWKV_QUAD_WIDTH = 256
RWKV_QUADS_PER_STEP = 16
LORA_PAD = 128

LANES = 128
SUBLANES = 8
MXU_DIM = 256
VMEM_LIMIT = 56 * 1024 * 1024


def _params(sem):
    return pltpu.CompilerParams(dimension_semantics=sem, vmem_limit_bytes=VMEM_LIMIT)


def _sigmoid(x):
    return 1.0 / (1.0 + jnp.exp(-x))


def _silu(x):
    return x * _sigmoid(x)


def _softplus(x):
    return jnp.maximum(x, 0.0) + jnp.log1p(jnp.exp(-jnp.abs(x)))


def _dot(a, b):
    return jnp.dot(a.astype(BF16), b.astype(BF16), preferred_element_type=F32)


def _dot_nt(a, b):
    return lax.dot_general(a.astype(BF16), b.astype(BF16), (((1,), (1,)), ((), ())),
                           preferred_element_type=F32)


def _dot_tn(a, b):
    return lax.dot_general(a.astype(BF16), b.astype(BF16), (((0,), (0,)), ((), ())),
                           preferred_element_type=F32)


def _split3(x):
    hi = x.astype(BF16)
    r1 = x - hi.astype(F32)
    mid = r1.astype(BF16)
    lo = (r1 - mid.astype(F32)).astype(BF16)
    return hi, mid, lo


def _dot_exact_rhs(a_f32, b_bf16):
    hi, mid, lo = _split3(a_f32)
    out = jnp.dot(hi, b_bf16, preferred_element_type=F32)
    out = out + jnp.dot(mid, b_bf16, preferred_element_type=F32)
    return out + jnp.dot(lo, b_bf16, preferred_element_type=F32)


def _dot_exact_lhs(a_bf16, b_f32):
    hi, mid, lo = _split3(b_f32)
    out = jnp.dot(a_bf16, hi, preferred_element_type=F32)
    out = out + jnp.dot(a_bf16, mid, preferred_element_type=F32)
    return out + jnp.dot(a_bf16, lo, preferred_element_type=F32)


def _rms_norm(x, g):
    return x * lax.rsqrt(jnp.mean(x * x, axis=-1, keepdims=True) + RMS_EPS) * g


def _tri(n, inclusive=True):
    r = lax.broadcasted_iota(jnp.int32, (n, n), 0)
    c = lax.broadcasted_iota(jnp.int32, (n, n), 1)
    return (c <= r) if inclusive else (c < r)


def _l0_in_kernel(x_ref, g_ref, w_ref, o_ref, h_ref):
    @pl.when(pl.program_id(1) == 0)
    def _():
        h_ref[...] = _rms_norm(x_ref[...], g_ref[...]).astype(BF16)

    o_ref[...] = jnp.dot(h_ref[...], w_ref[...], preferred_element_type=F32)


def _l0_in(x2, g, w_bf16, tm=1024, tn=1024):
    t, d = x2.shape
    n = w_bf16.shape[1]
    return pl.pallas_call(
        _l0_in_kernel,
        grid=(t // tm, n // tn),
        in_specs=[pl.BlockSpec((tm, d), lambda i, j: (i, 0)),
                  pl.BlockSpec((1, d), lambda i, j: (0, 0)),
                  pl.BlockSpec((d, tn), lambda i, j: (0, j))],
        out_specs=pl.BlockSpec((tm, tn), lambda i, j: (i, j)),
        out_shape=jax.ShapeDtypeStruct((t, n), F32),
        scratch_shapes=[pltpu.VMEM((tm, d), BF16)],
        compiler_params=_params(("arbitrary", "arbitrary")),
        name="l0_in",
    )(x2, g, w_bf16)


def _causal_conv(x, tail_ref, w, b):
    ts = x.shape[0]
    row = lax.broadcasted_iota(jnp.int32, (SUBLANES, x.shape[1]), 0)
    tail = tail_ref[...]
    y = b + x * w[CONV_WIDTH - 1:CONV_WIDTH, :]
    for k in range(1, CONV_WIDTH):
        rolled = pltpu.roll(x, k, 0)
        top = jnp.where(row < k, pltpu.roll(tail, k, 0), rolled[0:SUBLANES])
        xk = jnp.concatenate([top, rolled[SUBLANES:]], axis=0)
        y = y + xk * w[CONV_WIDTH - 1 - k:CONV_WIDTH - k, :]
    tail_ref[...] = x[ts - SUBLANES:ts]
    return y


def _lru_kernel(x_ref, z_ref, cw_ref, cb_ref, wa_ref, ba_ref, wx_ref, bx_ref, lam_ref, o_ref,
                xbuf, a_p, u_p, hl_s, al_s, hc_s, *, ts, nblk, pitch):
    s = pl.program_id(2)

    @pl.when(s == 0)
    def _():
        xbuf[...] = jnp.zeros_like(xbuf)
        hc_s[...] = jnp.zeros_like(hc_s)

    xc = _causal_conv(x_ref[...], xbuf, cw_ref[...], cb_ref[...])

    gr, gi = [], []
    for n in range(nblk):
        xn = xc[:, n * LRU_BLOCK:(n + 1) * LRU_BLOCK]
        gr.append(_dot(xn, wa_ref[n]))
        gi.append(_dot(xn, wx_ref[n]))
    r = _sigmoid(jnp.concatenate(gr, axis=1) + ba_ref[...])
    i = _sigmoid(jnp.concatenate(gi, axis=1) + bx_ref[...])
    log_a = -LRU_C * r * _softplus(-lam_ref[...])
    a = jnp.exp(log_a)
    u = jnp.sqrt(-jnp.tanh(log_a) * (a * a + 1.0)) * (i * xc)

    seg = ts // SUBLANES
    nslab = a.shape[1] // LANES
    for c in range(nslab):
        for j in range(SUBLANES):
            a_p[c, pl.ds(j * pitch, seg), :] = a[j * seg:(j + 1) * seg, c * LANES:(c + 1) * LANES]
            u_p[c, pl.ds(j * pitch, seg), :] = u[j * seg:(j + 1) * seg, c * LANES:(c + 1) * LANES]

    def local_scan(t, carry):
        out = []
        for c in range(nslab):
            h, acc = carry[c]
            at = a_p[c, pl.ds(t, SUBLANES, stride=pitch), :]
            ut = u_p[c, pl.ds(t, SUBLANES, stride=pitch), :]
            h = at * h + ut
            acc = at * acc
            hl_s[c, t] = h
            al_s[c, t] = acc
            out.append((h, acc))
        return tuple(out)

    init = tuple((jnp.zeros((SUBLANES, LANES), F32), jnp.ones((SUBLANES, LANES), F32))
                 for _ in range(nslab))
    ends = lax.fori_loop(0, seg, local_scan, init, unroll=4)

    row = lax.broadcasted_iota(jnp.int32, (SUBLANES, LANES), 0)
    for c in range(nslab):
        hs, ac = ends[c]
        for sft in (1, 2, 4):
            a_sh = jnp.where(row >= sft, pltpu.roll(ac, sft, 0), 1.0)
            h_sh = jnp.where(row >= sft, pltpu.roll(hs, sft, 0), 0.0)
            hs = ac * h_sh + hs
            ac = ac * a_sh
        carry = hc_s[:, c * LANES:(c + 1) * LANES]
        h_end = ac * carry + hs
        h_in = jnp.where(row == 0, carry, pltpu.roll(h_end, 1, 0))
        hc_s[:, c * LANES:(c + 1) * LANES] = h_end[SUBLANES - 1:SUBLANES, :]

        def fix(t, _, c=c, h_in=h_in):
            u_p[c, pl.ds(t, SUBLANES, stride=pitch), :] = hl_s[c, t] + al_s[c, t] * h_in
            return 0

        lax.fori_loop(0, seg, fix, 0, unroll=4)

    for c in range(nslab):
        for j in range(SUBLANES):
            rows = slice(j * seg, (j + 1) * seg)
            lanes = slice(c * LANES, (c + 1) * LANES)
            hj = u_p[c, pl.ds(j * pitch, seg), :]
            o_ref[rows, lanes] = (hj * _silu(z_ref[rows, lanes].astype(F32))).astype(BF16)


def _l0_lru(u, p, batch, seq, width, ts=512, tc=512):
    nblk = tc // LRU_BLOCK
    nct = width // tc
    nst = seq // ts
    seg = ts // SUBLANES
    pitch = seg + SUBLANES
    assert seg % (2 * SUBLANES) == 0
    nslab = tc // LANES
    zoff = (u.shape[1] // 2) // tc
    vec = lambda: pl.BlockSpec((1, tc), lambda b, c, s: (0, c))
    return pl.pallas_call(
        functools.partial(_lru_kernel, ts=ts, nblk=nblk, pitch=pitch),
        grid=(batch, nct, nst),
        in_specs=[pl.BlockSpec((ts, tc), lambda b, c, s: (b * nst + s, c)),
                  pl.BlockSpec((ts, tc), lambda b, c, s: (b * nst + s, zoff + c)),
                  pl.BlockSpec((CONV_WIDTH, tc), lambda b, c, s: (0, c)),
                  vec(),
                  pl.BlockSpec((nblk, LRU_BLOCK, LRU_BLOCK), lambda b, c, s: (c, 0, 0)),
                  vec(),
                  pl.BlockSpec((nblk, LRU_BLOCK, LRU_BLOCK), lambda b, c, s: (c, 0, 0)),
                  vec(), vec()],
        out_specs=pl.BlockSpec((ts, tc), lambda b, c, s: (b * nst + s, c)),
        out_shape=jax.ShapeDtypeStruct((batch * seq, width), BF16),
        scratch_shapes=[pltpu.VMEM((SUBLANES, tc), F32),
                        pltpu.VMEM((nslab, SUBLANES * pitch, LANES), F32),
                        pltpu.VMEM((nslab, SUBLANES * pitch, LANES), F32),
                        pltpu.VMEM((nslab, seg, SUBLANES, LANES), F32),
                        pltpu.VMEM((nslab, seg, SUBLANES, LANES), F32),
                        pltpu.VMEM((1, tc), F32)],
        compiler_params=_params(("arbitrary", "arbitrary", "arbitrary")),
        name="l0_lru",
    )(u, u, p["cw"], p["cb"], p["wa"], p["ba"], p["wx"], p["bx"], p["lam"])


def _bd4(xb, w_ref):
    nb, bw, _ = w_ref.shape
    return jnp.concatenate(
        [jnp.dot(xb[:, c * bw:(c + 1) * bw], w_ref[c], preferred_element_type=F32)
         for c in range(nb)], axis=1)


def _mlstm_kernel(x_ref, z_ref, cw_ref, cb_ref, wq_ref, wk_ref, wv_ref, wg_ref, bg_ref,
                  skip_ref, gn_ref, o_ref, xbuf, c_s, n_s, m_s, *, hd):
    L = MLSTM_CHUNK
    H = MLSTM_HEADS
    s = pl.program_id(1)

    @pl.when(s == 0)
    def _():
        xbuf[...] = jnp.zeros_like(xbuf)
        c_s[...] = jnp.zeros_like(c_s)
        n_s[...] = jnp.zeros_like(n_s)
        m_s[...] = jnp.zeros_like(m_s)

    x = x_ref[...]
    xmc = _silu(_causal_conv(x, xbuf, cw_ref[...], cb_ref[...]))

    xmc_b = xmc.astype(BF16)
    q = _bd4(xmc_b, wq_ref)
    k = _bd4(xmc_b, wk_ref)
    v = _bd4(x.astype(BF16), wv_ref)

    g = _dot(q, wg_ref[0]) + _dot(k, wg_ref[1]) + _dot(v, wg_ref[2]) + bg_ref[...]
    lane = lax.broadcasted_iota(jnp.int32, (L, LANES), 1)
    lf = jnp.minimum(g, 0.0) - jnp.log1p(jnp.exp(-jnp.abs(g)))
    lf = jnp.where((lane >= H) & (lane < 2 * H), lf, 0.0)
    csum = _dot_exact_lhs(_tri(L).astype(BF16), lf)
    gb = jnp.where(lane < H, g, csum)
    gbt = gb.T

    causal = _tri(L)
    kscale = hd ** -0.5
    heads = range(H)
    sls = [slice(h * hd, (h + 1) * hd) for h in heads]
    qb = [q[:, sl].astype(BF16) for sl in sls]
    kh = [k[:, sl] * kscale for sl in sls]
    vb = [v[:, sl].astype(BF16) for sl in sls]
    s_raw = [lax.dot_general(qb[h], kh[h].astype(BF16), (((1,), (1,)), ((), ())),
                             preferred_element_type=F32) for h in heads]
    qc = [jnp.dot(qb[h], c_s[h].astype(BF16), preferred_element_type=F32) for h in heads]

    wgt, inter, m_t, decay, kw, m_new = [], [], [], [], [], []
    for h in heads:
        b_col = gb[:, H + h:H + h + 1]
        i_col = gb[:, h:h + 1]
        b_row = gbt[H + h:H + h + 1, :]
        i_row = gbt[h:h + 1, :]
        m_prev = m_s[h][0:1, 0:1]
        dm = jnp.where(causal, b_col - b_row + i_row, -1e30)
        m_inter = b_col + m_prev
        mt = jnp.maximum(jnp.max(dm, axis=-1, keepdims=True), m_inter)
        wgt.append(jnp.exp(dm - mt))
        inter.append(jnp.exp(m_inter - mt))
        m_t.append(mt)
        b_last = b_col[L - 1:L, :]
        g_row = b_last - b_row + i_row
        g_col = b_last - b_col + i_col
        mn = jnp.maximum(b_last + m_prev, jnp.max(g_row, axis=-1, keepdims=True))
        m_new.append(mn)
        decay.append(jnp.exp(b_last + m_prev - mn))
        kw.append(kh[h] * jnp.exp(g_col - mn))

    upd = [lax.dot_general(kw[h].astype(BF16), vb[h], (((0,), (0,)), ((), ())),
                           preferred_element_type=F32) for h in heads]
    scores = [s_raw[h] * wgt[h] for h in heads]
    num = [jnp.dot(scores[h].astype(BF16), vb[h], preferred_element_type=F32)
           + inter[h] * qc[h] for h in heads]

    ys = []
    for h in heads:
        nvec = n_s[h]
        den = (jnp.sum(scores[h], axis=-1, keepdims=True)
               + inter[h] * jnp.sum(q[:, sls[h]] * nvec, axis=-1, keepdims=True))
        hout = num[h] / jnp.maximum(jnp.abs(den), jnp.exp(-m_t[h]))
        c_s[h] = decay[h] * c_s[h] + upd[h]
        n_s[h] = decay[h] * nvec + jnp.sum(kw[h], axis=0, keepdims=True)
        m_s[h] = jnp.broadcast_to(m_new[h], (SUBLANES, LANES))
        mu = jnp.mean(hout, axis=-1, keepdims=True)
        cen = hout - mu
        var = jnp.mean(cen * cen, axis=-1, keepdims=True)
        ys.append(cen * lax.rsqrt(var + MLSTM_GN_EPS))
    y = jnp.concatenate(ys, axis=1) * gn_ref[...] + skip_ref[...] * xmc
    o_ref[...] = (y * _silu(z_ref[...].astype(F32))).astype(BF16)


def _l0_mlstm(u, p, batch, seq, width, x_col, z_col):
    L = MLSTM_CHUNK
    nst = seq // L
    hd = width // MLSTM_HEADS
    full = lambda shape: pl.BlockSpec(shape, lambda b, s: (0,) * len(shape))
    return pl.pallas_call(
        functools.partial(_mlstm_kernel, hd=hd),
        grid=(batch, nst),
        in_specs=[pl.BlockSpec((L, width), lambda b, s: (b * nst + s, x_col)),
                  pl.BlockSpec((L, width), lambda b, s: (b * nst + s, z_col)),
                  full((CONV_WIDTH, width)), full((1, width)),
                  full((width // MXU_DIM, MXU_DIM, MXU_DIM)),
                  full((width // MXU_DIM, MXU_DIM, MXU_DIM)),
                  full((width // MXU_DIM, MXU_DIM, MXU_DIM)),
                  full((3, width, LANES)), full((1, LANES)),
                  full((1, width)), full((1, width))],
        out_specs=pl.BlockSpec((L, width), lambda b, s: (b * nst + s, 0)),
        out_shape=jax.ShapeDtypeStruct((batch * seq, width), BF16),
        scratch_shapes=[pltpu.VMEM((SUBLANES, width), F32),
                        pltpu.VMEM((MLSTM_HEADS, hd, hd), F32),
                        pltpu.VMEM((MLSTM_HEADS, 1, hd), F32),
                        pltpu.VMEM((MLSTM_HEADS, SUBLANES, LANES), F32)],
        compiler_params=_params(("arbitrary", "arbitrary")),
        name="l0_mlstm",
    )(u, u, p["cw"], p["cb"], p["wq"], p["wk"], p["wv"], p["wg"], p["bg"], p["skip"], p["gn"])


def _out_kernel(*refs, nin):
    y_refs = refs[:nin]
    w_ref, g_ref, x_ref, o_ref = refs[nin:]
    acc = jnp.dot(y_refs[0][...], w_ref[0], preferred_element_type=F32)
    for n in range(1, nin):
        acc = acc + jnp.dot(y_refs[n][...], w_ref[n], preferred_element_type=F32)
    o_ref[...] = x_ref[...] + _rms_norm(acc, g_ref[...])


def _out_proj(ys, w_bf16, g, x2, tm=256):
    nin, kin, d = w_bf16.shape
    t = x2.shape[0]
    return pl.pallas_call(
        functools.partial(_out_kernel, nin=nin),
        grid=(t // tm,),
        in_specs=[pl.BlockSpec((tm, kin), lambda i: (i, 0)) for _ in range(nin)]
        + [pl.BlockSpec((nin, kin, d), lambda i: (0, 0, 0), pipeline_mode=pl.Buffered(1)),
           pl.BlockSpec((1, d), lambda i: (0, 0)),
           pl.BlockSpec((tm, d), lambda i: (i, 0))],
        out_specs=pl.BlockSpec((tm, d), lambda i: (i, 0)),
        out_shape=jax.ShapeDtypeStruct((t, d), F32),
        compiler_params=_params(("arbitrary",)),
        name="out_proj",
    )(*ys, w_bf16, g, x2)


def _shift_rows(x, last_ref, first):
    row = lax.broadcasted_iota(jnp.int32, x.shape, 0)
    last = jnp.where(first, 0.0, last_ref[...])
    prev = jnp.where(row == 0, last, pltpu.roll(x, 1, 0))
    last_ref[...] = x[x.shape[0] - 1:x.shape[0], :]
    return prev


def _l1_in_kernel(x_ref, g_ref, w_ref, muw_ref, mua_ref, w1_ref, a1_ref, mu_ref, o_ref,
                  lo_ref, h_ref, last_ref, lastu_ref, *, tiles_per_seq):
    i = pl.program_id(0)
    j = pl.program_id(1)
    first = i % tiles_per_seq == 0

    @pl.when(j == 0)
    def _():
        h = _rms_norm(x_ref[...], g_ref[...])
        h_ref[...] = h.astype(BF16)
        dh = _shift_rows(h, last_ref, first) - h
        xw = h + dh * muw_ref[...]
        xa = h + dh * mua_ref[...]
        lo_ref[:, 0:LORA_PAD] = jnp.tanh(_dot(xw, w1_ref[...]))
        lo_ref[:, LORA_PAD:2 * LORA_PAD] = _dot(xa, a1_ref[...])

    acc = jnp.dot(h_ref[...], w_ref[...], preferred_element_type=F32)
    o_ref[...] = acc + (_shift_rows(acc, lastu_ref.at[j], first) - acc) * mu_ref[...]


def _l1_in(x2, g, w_bf16, muw, mua, w1, a1, mu_rkv, seq, tm=1024, tn=1024):
    t, d = x2.shape
    n = w_bf16.shape[1]
    mu_rkv = _pad_cols(mu_rkv, n)
    c2 = lambda shape: pl.BlockSpec(shape, lambda i, j: (0, 0))
    return pl.pallas_call(
        functools.partial(_l1_in_kernel, tiles_per_seq=seq // tm),
        grid=(t // tm, n // tn),
        in_specs=[pl.BlockSpec((tm, d), lambda i, j: (i, 0)),
                  c2((1, d)),
                  pl.BlockSpec((d, tn), lambda i, j: (0, j)),
                  c2((1, d)), c2((1, d)), c2((d, LORA_PAD)), c2((d, LORA_PAD)),
                  pl.BlockSpec((1, tn), lambda i, j: (0, j))],
        out_specs=[pl.BlockSpec((tm, tn), lambda i, j: (i, j)),
                   pl.BlockSpec((tm, 2 * LORA_PAD), lambda i, j: (i, 0))],
        out_shape=[jax.ShapeDtypeStruct((t, n), F32),
                   jax.ShapeDtypeStruct((t, 2 * LORA_PAD), F32)],
        scratch_shapes=[pltpu.VMEM((tm, d), BF16), pltpu.VMEM((1, d), F32),
                        pltpu.VMEM((n // tn, 1, tn), F32)],
        compiler_params=_params(("arbitrary", "arbitrary")),
        name="l1_in",
    )(x2, g, w_bf16, muw, mua, w1, a1, mu_rkv)


def _bdot(a, b):
    return jnp.dot(a, b, preferred_element_type=F32)


def _rwkv_kernel(r_ref, k_ref, v_ref, z_ref, lo_ref, w0_ref, w2_ref,
                 a0_ref, a2_ref, kk_ref, ka_ref, rk_ref, gg_ref, gb_ref, o_ref, st_s):
    L = RWKV_CHUNK
    D = RWKV_HEAD_DIM
    QW = RWKV_QUAD_WIDTH
    HQ = QW // D
    NQ = r_ref.shape[1] // QW
    s = pl.program_id(2)

    @pl.when(s == 0)
    def _():
        st_s[...] = jnp.zeros_like(st_s)

    r = r_ref[...]
    k = k_ref[...]
    v = v_ref[...]

    lo = lo_ref[...]
    logw = -RWKV_DECAY_SCALE * _sigmoid(w0_ref[...] + _dot(lo[:, 0:LORA_PAD], w2_ref[...]))
    a = _sigmoid(a0_ref[...] + _dot(lo[:, LORA_PAD:2 * LORA_PAD], a2_ref[...]))

    seg = ((lax.broadcasted_iota(jnp.int32, (QW, QW), 0) // D)
           == (lax.broadcasted_iota(jnp.int32, (QW, QW), 1) // D))
    seg_b = seg.astype(BF16)
    qs = [slice(q * QW, (q + 1) * QW) for q in range(NQ)]

    def segsum(x):
        stacked = jnp.concatenate([x[:, sl] for sl in qs], axis=0).astype(BF16)
        res = _bdot(stacked, seg_b)
        return jnp.concatenate([res[q * L:(q + 1) * L] for q in range(NQ)], axis=1)

    kk = k * kk_ref[...]
    kk = kk * lax.rsqrt(jnp.maximum(segsum(kk * kk), 1e-24))
    k = k * (1.0 + (a - 1.0) * ka_ref[...])
    bonus = segsum(r * k * rk_ref[...]) * v

    tri_b = _tri(L).astype(BF16)
    lw_hi = logw.astype(BF16)
    lw_lo = (logw - lw_hi.astype(F32)).astype(BF16)
    cs = _bdot(tri_b, lw_hi) + _bdot(tri_b, lw_lo)
    p_in = jnp.exp(cs)
    p_inv = jnp.exp(-cs)
    p_ex = jnp.exp(cs - logw)
    p_last = p_in[L - 1:L, :]

    rt = r * p_in
    at = -kk * p_ex
    kt = k * p_inv
    bt = kk * a * p_inv

    per_tile = LANES // D
    lane = lax.broadcasted_iota(jnp.int32, (1, LANES), 1) // D
    hmb = [(lane == h).astype(BF16) for h in range(per_tile)]
    rowi = lax.broadcasted_iota(jnp.int32, (L, QW), 0)
    coli = lax.broadcasted_iota(jnp.int32, (L, QW), 1) % D
    strict = coli < rowi
    lower = coli <= rowi
    eye = (coli == rowi).astype(F32)
    zero_tile = jnp.zeros((L, LANES), BF16)

    def head_rows(mb):
        blocks = []
        for h in range(HQ):
            t = h // per_tile
            keep = mb[:, t * LANES:(t + 1) * LANES] * hmb[h % per_tile]
            blocks.append(jnp.concatenate(
                [keep if c == t else zero_tile for c in range(QW // LANES)], axis=1))
        return blocks

    def bd(m):
        return jnp.concatenate(head_rows(m.astype(BF16)), axis=0)

    def cat2(x, y):
        return jnp.concatenate([x, y], axis=0)

    xs, zs, vs, a_ab, a_ak, a_rb, a_rk = [], [], [], [], [], [], []
    for sl in qs:
        xq = cat2(at[:, sl], rt[:, sl]).astype(BF16)
        btb = bt[:, sl].astype(BF16)
        ktb = kt[:, sl].astype(BF16)
        ycat = jnp.concatenate(head_rows(btb) + head_rows(ktb), axis=0)
        aall = lax.dot_general(xq, ycat, (((1,), (1,)), ((), ())),
                               preferred_element_type=F32)
        xs.append(xq)
        zs.append(cat2(btb, ktb))
        vs.append(v[:, sl])
        a_ab.append(jnp.where(strict, aall[0:L, 0:QW], 0.0))
        a_ak.append(jnp.where(strict, aall[0:L, QW:2 * QW], 0.0))
        a_rb.append(jnp.where(lower, aall[L:2 * L, 0:QW], 0.0))
        a_rk.append(jnp.where(lower, aall[L:2 * L, QW:2 * QW], 0.0))

    def bdb(mb):
        return jnp.concatenate(head_rows(mb), axis=0)

    tinv = [eye + m for m in a_ab]
    pwb = [m.astype(BF16) for m in a_ab]
    pwb = [_bdot(pwb[q], bdb(pwb[q])).astype(BF16) for q in range(NQ)]
    sx = [_bdot(xs[q], st_s[q].astype(BF16)) for q in range(NQ)]
    av = [_bdot(cat2(a_ak[q], a_rk[q]).astype(BF16), bd(vs[q])) for q in range(NQ)]
    for _ in range(L.bit_length() - 3):
        both = [_bdot(cat2(tinv[q].astype(BF16), pwb[q]), bdb(pwb[q])) for q in range(NQ)]
        tinv = [tinv[q] + both[q][0:L] for q in range(NQ)]
        pwb = [both[q][L:2 * L].astype(BF16) for q in range(NQ)]
    tinv = [tinv[q] + _bdot(tinv[q].astype(BF16), bdb(pwb[q])) for q in range(NQ)]

    us = [_bdot(tinv[q].astype(BF16), bd(sx[q][0:L] + av[q][0:L])) for q in range(NQ)]
    ys = [sx[q][L:2 * L] + av[q][L:2 * L] + _bdot(a_rb[q].astype(BF16), bd(us[q]))
          for q in range(NQ)]

    for q, sl in enumerate(qs):
        upd = lax.dot_general(zs[q], cat2(us[q], vs[q]).astype(BF16),
                              (((0,), (0,)), ((), ())),
                              preferred_element_type=F32)
        cols = []
        for cb in range(QW // LANES):
            lo_l = q * QW + cb * LANES
            cols.append(jnp.broadcast_to(p_last[:, lo_l:lo_l + LANES], (LANES, LANES)).T)
        pcol = jnp.concatenate(cols, axis=0)
        pcol = jnp.concatenate([pcol] * (QW // LANES), axis=1)
        st_s[q] = pcol * (st_s[q] + jnp.where(seg, upd, 0.0))

    yv = jnp.concatenate(ys, axis=1)
    cen = yv - segsum(yv) * (1.0 / D)
    var = segsum(cen * cen) * (1.0 / D)
    y = cen * lax.rsqrt(var + RWKV_GN_EPS) * gg_ref[...] + gb_ref[...] + bonus
    o_ref[...] = (y * _silu(z_ref[...].astype(F32))).astype(BF16)


def _l1_rwkv(u, lo, p, batch, seq, width):
    L = RWKV_CHUNK
    QW = RWKV_QUAD_WIDTH
    wt = RWKV_QUADS_PER_STEP * QW
    ngrp = width // wt
    nst = seq // L
    vec = lambda: pl.BlockSpec((1, wt), lambda b, g, s: (0, g))
    col = lambda off: pl.BlockSpec((L, wt), lambda b, g, s: (b * nst + s, off + g))
    return pl.pallas_call(
        _rwkv_kernel,
        grid=(batch, ngrp, nst),
        in_specs=[col(0), col(ngrp), col(2 * ngrp), col(3 * ngrp),
                  pl.BlockSpec((L, 2 * LORA_PAD), lambda b, g, s: (b * nst + s, 0)),
                  vec(),
                  pl.BlockSpec((LORA_PAD, wt), lambda b, g, s: (0, g)),
                  vec(),
                  pl.BlockSpec((LORA_PAD, wt), lambda b, g, s: (0, g)),
                  vec(), vec(), vec(), vec(), vec()],
        out_specs=pl.BlockSpec((L, wt), lambda b, g, s: (b * nst + s, g)),
        out_shape=jax.ShapeDtypeStruct((batch * seq, width), BF16),
        scratch_shapes=[pltpu.VMEM((RWKV_QUADS_PER_STEP, QW, QW), F32)],
        compiler_params=_params(("arbitrary", "arbitrary", "arbitrary")),
        name="l1_rwkv",
    )(u, u, u, u, lo, p["w0"], p["w2"], p["a0"], p["a2"],
      p["kk"], p["ka"], p["rk"], p["gg"], p["gb"])


def _bd4_dense(w):
    nb, bs, _ = w.shape
    per = MXU_DIM // bs
    wb = w.reshape(nb // per, per, bs, bs)
    dense = jnp.einsum('cnij,nm->cnimj', wb, jnp.eye(per, dtype=w.dtype))
    return dense.reshape(nb // per, MXU_DIM, MXU_DIM).astype(BF16)


def _row(v):
    return v.reshape(1, -1).astype(F32)


def _pad_rows(w, rows):
    return jnp.pad(w, ((0, rows - w.shape[0]), (0, 0)))


def _pad_cols(w, cols):
    return jnp.pad(w, ((0, 0), (0, cols - w.shape[1])))


def kernel(x, l0_norm_pre, l0_w_in, l0_lru_conv_w, l0_lru_conv_b, l0_lru_wa, l0_lru_ba,
           l0_lru_wx, l0_lru_bx, l0_lru_lambda, l0_m_conv_w, l0_m_conv_b, l0_m_wq, l0_m_wk,
           l0_m_wv, l0_m_wi, l0_m_bi, l0_m_wf, l0_m_bf, l0_m_skip, l0_m_gn, l0_w_out,
           l0_norm_post, l1_norm_pre, l1_w_in, l1_mu_rkv, l1_mu_w, l1_mu_a, l1_w0, l1_w1,
           l1_w2, l1_a0, l1_a1, l1_a2, l1_k_k, l1_k_a, l1_r_k, l1_gn_g, l1_gn_b, l1_w_out,
           l1_norm_post):
    batch, seq, d = x.shape
    x2 = x.reshape(batch * seq, d).astype(F32)
    rw = l0_lru_lambda.shape[0]
    mw = l0_m_skip.shape[0]
    ww = l1_w0.shape[0]
    assert rw == mw, "column blocking assumes equal head-group widths"

    w_in0 = jnp.concatenate([l0_w_in[:, 0:rw], l0_w_in[:, 2 * rw:2 * rw + mw],
                             l0_w_in[:, rw:2 * rw], l0_w_in[:, 2 * rw + mw:]], axis=1).astype(BF16)
    u0 = _l0_in(x2, _row(l0_norm_pre), w_in0)

    lru_p = dict(cw=l0_lru_conv_w.astype(F32), cb=_row(l0_lru_conv_b),
                 wa=l0_lru_wa.astype(BF16), ba=_row(l0_lru_ba),
                 wx=l0_lru_wx.astype(BF16), bx=_row(l0_lru_bx), lam=_row(l0_lru_lambda))
    yr = _l0_lru(u0, lru_p, batch, seq, rw)

    wg = jnp.concatenate([l0_m_wi, l0_m_wf], axis=1)
    wg = _pad_cols(wg, LANES).reshape(3, mw, LANES).astype(BF16)
    bg = _pad_cols(jnp.concatenate([l0_m_bi, l0_m_bf]).reshape(1, -1), LANES).astype(F32)
    m_p = dict(cw=l0_m_conv_w.astype(F32), cb=_row(l0_m_conv_b),
               wq=_bd4_dense(l0_m_wq), wk=_bd4_dense(l0_m_wk),
               wv=_bd4_dense(l0_m_wv), wg=wg, bg=bg,
               skip=_row(l0_m_skip), gn=_row(l0_m_gn))
    ym = _l0_mlstm(u0, m_p, batch, seq, mw, x_col=rw // mw, z_col=(rw + mw + rw) // mw)

    w_out0 = l0_w_out.astype(BF16).reshape(2, rw, d)
    x1 = _out_proj([yr, ym], w_out0, _row(l0_norm_post), x2)

    w1 = _pad_cols(l1_w1, LORA_PAD).astype(BF16)
    a1 = _pad_cols(l1_a1, LORA_PAD).astype(BF16)
    u1, lo1 = _l1_in(x1, _row(l1_norm_pre), l1_w_in.astype(BF16), _row(l1_mu_w),
                           _row(l1_mu_a), w1, a1, _row(l1_mu_rkv), seq)
    r_p = dict(w0=_row(l1_w0),
               w2=_pad_rows(l1_w2, LORA_PAD).astype(BF16), a0=_row(l1_a0),
               a2=_pad_rows(l1_a2, LORA_PAD).astype(BF16), kk=_row(l1_k_k), ka=_row(l1_k_a),
               rk=_row(l1_r_k), gg=_row(l1_gn_g), gb=_row(l1_gn_b))
    y1 = _l1_rwkv(u1, lo1, r_p, batch, seq, ww)
    out = _out_proj([y1], l1_w_out.astype(BF16).reshape(1, ww, d), _row(l1_norm_post), x1)
    return out.reshape(batch, seq, d).astype(x.dtype)
```

```python
import functools

import jax
import jax.numpy as jnp
import numpy as np
from jax import lax
from jax.experimental import pallas as pl
from jax.experimental.pallas import tpu as pltpu

F32 = jnp.float32
BF16 = jnp.bfloat16

RMS_EPS = 1e-6
CONV_WIDTH = 4
LRU_C = 8.0
LRU_BLOCK = 128
MLSTM_HEADS = 8
MLSTM_QKV_BLOCK = 4
MLSTM_CHUNK = 128
MLSTM_GN_EPS = 1e-6
RWKV_HEAD_DIM = 64
RWKV_GN_EPS = 64e-5
RWKV_CHUNK = 64
RWKV_DECAY_SCALE = 0.6065306597126334
RWKV_QUAD_WIDTH = 256
RWKV_QUADS_PER_STEP = 16
LORA_PAD = 128

LANES = 128
SUBLANES = 8
MXU_DIM = 256
VMEM_LIMIT = 56 * 1024 * 1024


def _params(sem):
    return pltpu.CompilerParams(dimension_semantics=sem, vmem_limit_bytes=VMEM_LIMIT)


def _sigmoid(x):
    return 1.0 / (1.0 + jnp.exp(-x))


def _silu(x):
    return x * _sigmoid(x)


def _softplus(x):
    return jnp.maximum(x, 0.0) + jnp.log1p(jnp.exp(-jnp.abs(x)))


def _dot(a, b):
    return jnp.dot(a.astype(BF16), b.astype(BF16), preferred_element_type=F32)


def _dot_nt(a, b):
    return lax.dot_general(a.astype(BF16), b.astype(BF16), (((1,), (1,)), ((), ())),
                           preferred_element_type=F32)


def _dot_tn(a, b):
    return lax.dot_general(a.astype(BF16), b.astype(BF16), (((0,), (0,)), ((), ())),
                           preferred_element_type=F32)


def _split3(x):
    hi = x.astype(BF16)
    r1 = x - hi.astype(F32)
    mid = r1.astype(BF16)
    lo = (r1 - mid.astype(F32)).astype(BF16)
    return hi, mid, lo


def _dot_exact_rhs(a_f32, b_bf16):
    hi, mid, lo = _split3(a_f32)
    out = jnp.dot(hi, b_bf16, preferred_element_type=F32)
    out = out + jnp.dot(mid, b_bf16, preferred_element_type=F32)
    return out + jnp.dot(lo, b_bf16, preferred_element_type=F32)


def _dot_exact_lhs(a_bf16, b_f32):
    hi, mid, lo = _split3(b_f32)
    out = jnp.dot(a_bf16, hi, preferred_element_type=F32)
    out = out + jnp.dot(a_bf16, mid, preferred_element_type=F32)
    return out + jnp.dot(a_bf16, lo, preferred_element_type=F32)


def _rms_norm(x, g):
    return x * lax.rsqrt(jnp.mean(x * x, axis=-1, keepdims=True) + RMS_EPS) * g


def _tri(n, inclusive=True):
    r = lax.broadcasted_iota(jnp.int32, (n, n), 0)
    c = lax.broadcasted_iota(jnp.int32, (n, n), 1)
    return (c <= r) if inclusive else (c < r)


def _l0_in_kernel(x_ref, g_ref, w_ref, o_ref, h_ref):
    @pl.when(pl.program_id(1) == 0)
    def _():
        h_ref[...] = _rms_norm(x_ref[...], g_ref[...]).astype(BF16)

    o_ref[...] = jnp.dot(h_ref[...], w_ref[...], preferred_element_type=F32)


def _l0_in(x2, g, w_bf16, tm=1024, tn=1024):
    t, d = x2.shape
    n = w_bf16.shape[1]
    return pl.pallas_call(
        _l0_in_kernel,
        grid=(t // tm, n // tn),
        in_specs=[pl.BlockSpec((tm, d), lambda i, j: (i, 0)),
                  pl.BlockSpec((1, d), lambda i, j: (0, 0)),
                  pl.BlockSpec((d, tn), lambda i, j: (0, j))],
        out_specs=pl.BlockSpec((tm, tn), lambda i, j: (i, j)),
        out_shape=jax.ShapeDtypeStruct((t, n), F32),
        scratch_shapes=[pltpu.VMEM((tm, d), BF16)],
        compiler_params=_params(("arbitrary", "arbitrary")),
        name="l0_in",
    )(x2, g, w_bf16)


def _causal_conv(x, tail_ref, w, b):
    ts = x.shape[0]
    row = lax.broadcasted_iota(jnp.int32, (SUBLANES, x.shape[1]), 0)
    tail = tail_ref[...]
    y = b + x * w[CONV_WIDTH - 1:CONV_WIDTH, :]
    for k in range(1, CONV_WIDTH):
        rolled = pltpu.roll(x, k, 0)
        top = jnp.where(row < k, pltpu.roll(tail, k, 0), rolled[0:SUBLANES])
        xk = jnp.concatenate([top, rolled[SUBLANES:]], axis=0)
        y = y + xk * w[CONV_WIDTH - 1 - k:CONV_WIDTH - k, :]
    tail_ref[...] = x[ts - SUBLANES:ts]
    return y


def _lru_kernel(x_ref, z_ref, cw_ref, cb_ref, wa_ref, ba_ref, wx_ref, bx_ref, lam_ref, o_ref,
                xbuf, a_p, u_p, hl_s, al_s, hc_s, *, ts, nblk, pitch):
    s = pl.program_id(2)

    @pl.when(s == 0)
    def _():
        xbuf[...] = jnp.zeros_like(xbuf)
        hc_s[...] = jnp.zeros_like(hc_s)

    xc = _causal_conv(x_ref[...], xbuf, cw_ref[...], cb_ref[...])

    gr, gi = [], []
    for n in range(nblk):
        xn = xc[:, n * LRU_BLOCK:(n + 1) * LRU_BLOCK]
        gr.append(_dot(xn, wa_ref[n]))
        gi.append(_dot(xn, wx_ref[n]))
    r = _sigmoid(jnp.concatenate(gr, axis=1) + ba_ref[...])
    i = _sigmoid(jnp.concatenate(gi, axis=1) + bx_ref[...])
    log_a = -LRU_C * r * _softplus(-lam_ref[...])
    a = jnp.exp(log_a)
    u = jnp.sqrt(-jnp.tanh(log_a) * (a * a + 1.0)) * (i * xc)

    seg = ts // SUBLANES
    nslab = a.shape[1] // LANES
    for c in range(nslab):
        for j in range(SUBLANES):
            a_p[c, pl.ds(j * pitch, seg), :] = a[j * seg:(j + 1) * seg, c * LANES:(c + 1) * LANES]
            u_p[c, pl.ds(j * pitch, seg), :] = u[j * seg:(j + 1) * seg, c * LANES:(c + 1) * LANES]

    def local_scan(t, carry):
        out = []
        for c in range(nslab):
            h, acc = carry[c]
            at = a_p[c, pl.ds(t, SUBLANES, stride=pitch), :]
            ut = u_p[c, pl.ds(t, SUBLANES, stride=pitch), :]
            h = at * h + ut
            acc = at * acc
            hl_s[c, t] = h
            al_s[c, t] = acc
            out.append((h, acc))
        return tuple(out)

    init = tuple((jnp.zeros((SUBLANES, LANES), F32), jnp.ones((SUBLANES, LANES), F32))
                 for _ in range(nslab))
    ends = lax.fori_loop(0, seg, local_scan, init, unroll=4)

    row = lax.broadcasted_iota(jnp.int32, (SUBLANES, LANES), 0)
    for c in range(nslab):
        hs, ac = ends[c]
        for sft in (1, 2, 4):
            a_sh = jnp.where(row >= sft, pltpu.roll(ac, sft, 0), 1.0)
            h_sh = jnp.where(row >= sft, pltpu.roll(hs, sft, 0), 0.0)
            hs = ac * h_sh + hs
            ac = ac * a_sh
        carry = hc_s[:, c * LANES:(c + 1) * LANES]
        h_end = ac * carry + hs
        h_in = jnp.where(row == 0, carry, pltpu.roll(h_end, 1, 0))
        hc_s[:, c * LANES:(c + 1) * LANES] = h_end[SUBLANES - 1:SUBLANES, :]

        def fix(t, _, c=c, h_in=h_in):
            u_p[c, pl.ds(t, SUBLANES, stride=pitch), :] = hl_s[c, t] + al_s[c, t] * h_in
            return 0

        lax.fori_loop(0, seg, fix, 0, unroll=4)

    for c in range(nslab):
        for j in range(SUBLANES):
            rows = slice(j * seg, (j + 1) * seg)
            lanes = slice(c * LANES, (c + 1) * LANES)
            hj = u_p[c, pl.ds(j * pitch, seg), :]
            o_ref[rows, lanes] = (hj * _silu(z_ref[rows, lanes].astype(F32))).astype(BF16)


def _l0_lru(u, p, batch, seq, width, ts=512, tc=512):
    nblk = tc // LRU_BLOCK
    nct = width // tc
    nst = seq // ts
    seg = ts // SUBLANES
    pitch = seg + SUBLANES
    assert seg % (2 * SUBLANES) == 0
    nslab = tc // LANES
    zoff = width // tc
    vec = lambda: pl.BlockSpec((1, tc), lambda b, c, s: (0, c))
    return pl.pallas_call(
        functools.partial(_lru_kernel, ts=ts, nblk=nblk, pitch=pitch),
        grid=(batch, nct, nst),
        in_specs=[pl.BlockSpec((ts, tc), lambda b, c, s: (b * nst + s, c)),
                  pl.BlockSpec((ts, tc), lambda b, c, s: (b * nst + s, zoff + c)),
                  pl.BlockSpec((CONV_WIDTH, tc), lambda b, c, s: (0, c)),
                  vec(),
                  pl.BlockSpec((nblk, LRU_BLOCK, LRU_BLOCK), lambda b, c, s: (c, 0, 0)),
                  vec(),
                  pl.BlockSpec((nblk, LRU_BLOCK, LRU_BLOCK), lambda b, c, s: (c, 0, 0)),
                  vec(), vec()],
        out_specs=pl.BlockSpec((ts, tc), lambda b, c, s: (b * nst + s, c)),
        out_shape=jax.ShapeDtypeStruct((batch * seq, width), BF16),
        scratch_shapes=[pltpu.VMEM((SUBLANES, tc), F32),
                        pltpu.VMEM((nslab, SUBLANES * pitch, LANES), F32),
                        pltpu.VMEM((nslab, SUBLANES * pitch, LANES), F32),
                        pltpu.VMEM((nslab, seg, SUBLANES, LANES), F32),
                        pltpu.VMEM((nslab, seg, SUBLANES, LANES), F32),
                        pltpu.VMEM((1, tc), F32)],
        compiler_params=_params(("arbitrary", "arbitrary", "arbitrary")),
        name="l0_lru",
    )(u, u, p["cw"], p["cb"], p["wa"], p["ba"], p["wx"], p["bx"], p["lam"])


def _bd4(xb, w_ref):
    nb, bw, _ = w_ref.shape
    return jnp.concatenate(
        [jnp.dot(xb[:, c * bw:(c + 1) * bw], w_ref[c], preferred_element_type=F32)
         for c in range(nb)], axis=1)


def _mlstm_kernel(x_ref, z_ref, cw_ref, cb_ref, wq_ref, wk_ref, wv_ref, wg_ref, bg_ref,
                  skip_ref, gn_ref, o_ref, xbuf, c_s, n_s, m_s, *, hd):
    L = MLSTM_CHUNK
    H = MLSTM_HEADS
    s = pl.program_id(1)

    @pl.when(s == 0)
    def _():
        xbuf[...] = jnp.zeros_like(xbuf)
        c_s[...] = jnp.zeros_like(c_s)
        n_s[...] = jnp.zeros_like(n_s)
        m_s[...] = jnp.zeros_like(m_s)

    x = x_ref[...]
    xmc = _silu(_causal_conv(x, xbuf, cw_ref[...], cb_ref[...]))

    xmc_b = xmc.astype(BF16)
    q = _bd4(xmc_b, wq_ref)
    k = _bd4(xmc_b, wk_ref)
    v = _bd4(x.astype(BF16), wv_ref)

    g = _dot(q, wg_ref[0]) + _dot(k, wg_ref[1]) + _dot(v, wg_ref[2]) + bg_ref[...]
    lane = lax.broadcasted_iota(jnp.int32, (L, LANES), 1)
    lf = jnp.minimum(g, 0.0) - jnp.log1p(jnp.exp(-jnp.abs(g)))
    lf = jnp.where((lane >= H) & (lane < 2 * H), lf, 0.0)
    csum = _dot_exact_lhs(_tri(L).astype(BF16), lf)
    gb = jnp.where(lane < H, g, csum)
    gbt = gb.T

    causal = _tri(L)
    kscale = hd ** -0.5
    heads = range(H)
    sls = [slice(h * hd, (h + 1) * hd) for h in heads]
    qb = [q[:, sl].astype(BF16) for sl in sls]
    kh = [k[:, sl] * kscale for sl in sls]
    vb = [v[:, sl].astype(BF16) for sl in sls]
    s_raw = [lax.dot_general(qb[h], kh[h].astype(BF16), (((1,), (1,)), ((), ())),
                             preferred_element_type=F32) for h in heads]
    qc = [jnp.dot(qb[h], c_s[h].astype(BF16), preferred_element_type=F32) for h in heads]

    wgt, inter, m_t, decay, kw, m_new = [], [], [], [], [], []
    for h in heads:
        b_col = gb[:, H + h:H + h + 1]
        i_col = gb[:, h:h + 1]
        b_row = gbt[H + h:H + h + 1, :]
        i_row = gbt[h:h + 1, :]
        m_prev = m_s[h][0:1, 0:1]
        dm = jnp.where(causal, b_col - b_row + i_row, -1e30)
        m_inter = b_col + m_prev
        mt = jnp.maximum(jnp.max(dm, axis=-1, keepdims=True), m_inter)
        wgt.append(jnp.exp(dm - mt))
        inter.append(jnp.exp(m_inter - mt))
        m_t.append(mt)
        b_last = b_col[L - 1:L, :]
        g_row = b_last - b_row + i_row
        g_col = b_last - b_col + i_col
        mn = jnp.maximum(b_last + m_prev, jnp.max(g_row, axis=-1, keepdims=True))
        m_new.append(mn)
        decay.append(jnp.exp(b_last + m_prev - mn))
        kw.append(kh[h] * jnp.exp(g_col - mn))

    upd = [lax.dot_general(kw[h].astype(BF16), vb[h], (((0,), (0,)), ((), ())),
                           preferred_element_type=F32) for h in heads]
    scores = [s_raw[h] * wgt[h] for h in heads]
    num = [jnp.dot(scores[h].astype(BF16), vb[h], preferred_element_type=F32)
           + inter[h] * qc[h] for h in heads]

    ys = []
    for h in heads:
        nvec = n_s[h]
        den = (jnp.sum(scores[h], axis=-1, keepdims=True)
               + inter[h] * jnp.sum(q[:, sls[h]] * nvec, axis=-1, keepdims=True))
        hout = num[h] / jnp.maximum(jnp.abs(den), jnp.exp(-m_t[h]))
        c_s[h] = decay[h] * c_s[h] + upd[h]
        n_s[h] = decay[h] * nvec + jnp.sum(kw[h], axis=0, keepdims=True)
        m_s[h] = jnp.broadcast_to(m_new[h], (SUBLANES, LANES))
        mu = jnp.mean(hout, axis=-1, keepdims=True)
        cen = hout - mu
        var = jnp.mean(cen * cen, axis=-1, keepdims=True)
        ys.append(cen * lax.rsqrt(var + MLSTM_GN_EPS))
    y = jnp.concatenate(ys, axis=1) * gn_ref[...] + skip_ref[...] * xmc
    o_ref[...] = (y * _silu(z_ref[...].astype(F32))).astype(BF16)


def _l0_mlstm(u, p, batch, seq, width, x_col, z_col):
    L = MLSTM_CHUNK
    nst = seq // L
    hd = width // MLSTM_HEADS
    full = lambda shape: pl.BlockSpec(shape, lambda b, s: (0,) * len(shape))
    return pl.pallas_call(
        functools.partial(_mlstm_kernel, hd=hd),
        grid=(batch, nst),
        in_specs=[pl.BlockSpec((L, width), lambda b, s: (b * nst + s, x_col)),
                  pl.BlockSpec((L, width), lambda b, s: (b * nst + s, z_col)),
                  full((CONV_WIDTH, width)), full((1, width)),
                  full((width // MXU_DIM, MXU_DIM, MXU_DIM)),
                  full((width // MXU_DIM, MXU_DIM, MXU_DIM)),
                  full((width // MXU_DIM, MXU_DIM, MXU_DIM)),
                  full((3, width, LANES)), full((1, LANES)),
                  full((1, width)), full((1, width))],
        out_specs=pl.BlockSpec((L, width), lambda b, s: (b * nst + s, 0)),
        out_shape=jax.ShapeDtypeStruct((batch * seq, width), BF16),
        scratch_shapes=[pltpu.VMEM((SUBLANES, width), F32),
                        pltpu.VMEM((MLSTM_HEADS, hd, hd), F32),
                        pltpu.VMEM((MLSTM_HEADS, 1, hd), F32),
                        pltpu.VMEM((MLSTM_HEADS, SUBLANES, LANES), F32)],
        compiler_params=_params(("arbitrary", "arbitrary")),
        name="l0_mlstm",
    )(u, u, p["cw"], p["cb"], p["wq"], p["wk"], p["wv"], p["wg"], p["bg"], p["skip"], p["gn"])


def _out_kernel(*refs, nin):
    y_refs = refs[:nin]
    w_ref, g_ref, x_ref, o_ref = refs[nin:]
    acc = jnp.dot(y_refs[0][...], w_ref[0], preferred_element_type=F32)
    for n in range(1, nin):
        acc = acc + jnp.dot(y_refs[n][...], w_ref[n], preferred_element_type=F32)
    o_ref[...] = x_ref[...] + _rms_norm(acc, g_ref[...])


def _out_proj(ys, w_bf16, g, x2, tm=256):
    nin, kin, d = w_bf16.shape
    t = x2.shape[0]
    return pl.pallas_call(
        functools.partial(_out_kernel, nin=nin),
        grid=(t // tm,),
        in_specs=[pl.BlockSpec((tm, kin), lambda i: (i, 0)) for _ in range(nin)]
        + [pl.BlockSpec((nin, kin, d), lambda i: (0, 0, 0), pipeline_mode=pl.Buffered(1)),
           pl.BlockSpec((1, d), lambda i: (0, 0)),
           pl.BlockSpec((tm, d), lambda i: (i, 0))],
        out_specs=pl.BlockSpec((tm, d), lambda i: (i, 0)),
        out_shape=jax.ShapeDtypeStruct((t, d), F32),
        compiler_params=_params(("arbitrary",)),
        name="out_proj",
    )(*ys, w_bf16, g, x2)


def _shift_rows(x, last_ref, first):
    row = lax.broadcasted_iota(jnp.int32, x.shape, 0)
    last = jnp.where(first, 0.0, last_ref[...])
    prev = jnp.where(row == 0, last, pltpu.roll(x, 1, 0))
    last_ref[...] = x[x.shape[0] - 1:x.shape[0], :]
    return prev


def _l1_in_kernel(x_ref, g_ref, w_ref, muw_ref, mua_ref, w1_ref, a1_ref, mu_ref, o_ref,
                  lo_ref, h_ref, last_ref, lastu_ref, *, tiles_per_seq):
    i = pl.program_id(0)
    j = pl.program_id(1)
    first = i % tiles_per_seq == 0

    @pl.when(j == 0)
    def _():
        h = _rms_norm(x_ref[...], g_ref[...])
        h_ref[...] = h.astype(BF16)
        dh = _shift_rows(h, last_ref, first) - h
        xw = h + dh * muw_ref[...]
        xa = h + dh * mua_ref[...]
        lo_ref[:, 0:LORA_PAD] = jnp.tanh(_dot(xw, w1_ref[...]))
        lo_ref[:, LORA_PAD:2 * LORA_PAD] = _dot(xa, a1_ref[...])

    acc = jnp.dot(h_ref[...], w_ref[...], preferred_element_type=F32)
    o_ref[...] = acc + (_shift_rows(acc, lastu_ref.at[j], first) - acc) * mu_ref[...]


def _l1_in(x2, g, w_bf16, muw, mua, w1, a1, mu_rkv, seq, tm=1024, tn=1024):
    t, d = x2.shape
    n = w_bf16.shape[1]
    mu_rkv = _pad_cols(mu_rkv, n)
    c2 = lambda shape: pl.BlockSpec(shape, lambda i, j: (0, 0))
    return pl.pallas_call(
        functools.partial(_l1_in_kernel, tiles_per_seq=seq // tm),
        grid=(t // tm, n // tn),
        in_specs=[pl.BlockSpec((tm, d), lambda i, j: (i, 0)),
                  c2((1, d)),
                  pl.BlockSpec((d, tn), lambda i, j: (0, j)),
                  c2((1, d)), c2((1, d)), c2((d, LORA_PAD)), c2((d, LORA_PAD)),
                  pl.BlockSpec((1, tn), lambda i, j: (0, j))],
        out_specs=[pl.BlockSpec((tm, tn), lambda i, j: (i, j)),
                   pl.BlockSpec((tm, 2 * LORA_PAD), lambda i, j: (i, 0))],
        out_shape=[jax.ShapeDtypeStruct((t, n), F32),
                   jax.ShapeDtypeStruct((t, 2 * LORA_PAD), F32)],
        scratch_shapes=[pltpu.VMEM((tm, d), BF16), pltpu.VMEM((1, d), F32),
                        pltpu.VMEM((n // tn, 1, tn), F32)],
        compiler_params=_params(("arbitrary", "arbitrary")),
        name="l1_in",
    )(x2, g, w_bf16, muw, mua, w1, a1, mu_rkv)


def _bdot(a, b):
    return jnp.dot(a, b, preferred_element_type=F32)


def _rwkv_kernel(r_ref, k_ref, v_ref, z_ref, lo_ref, w0_ref, w2_ref,
                 a0_ref, a2_ref, kk_ref, ka_ref, rk_ref, gg_ref, gb_ref, o_ref, st_s):
    L = RWKV_CHUNK
    D = RWKV_HEAD_DIM
    QW = RWKV_QUAD_WIDTH
    HQ = QW // D
    NQ = r_ref.shape[1] // QW
    s = pl.program_id(2)

    @pl.when(s == 0)
    def _():
        st_s[...] = jnp.zeros_like(st_s)

    r = r_ref[...]
    k = k_ref[...]
    v = v_ref[...]

    lo = lo_ref[...]
    logw = -RWKV_DECAY_SCALE * _sigmoid(w0_ref[...] + _dot(lo[:, 0:LORA_PAD], w2_ref[...]))
    a = _sigmoid(a0_ref[...] + _dot(lo[:, LORA_PAD:2 * LORA_PAD], a2_ref[...]))

    seg = ((lax.broadcasted_iota(jnp.int32, (QW, QW), 0) // D)
           == (lax.broadcasted_iota(jnp.int32, (QW, QW), 1) // D))
    seg_b = seg.astype(BF16)
    qs = [slice(q * QW, (q + 1) * QW) for q in range(NQ)]

    def segsum(x):
        stacked = jnp.concatenate([x[:, sl] for sl in qs], axis=0).astype(BF16)
        res = _bdot(stacked, seg_b)
        return jnp.concatenate([res[q * L:(q + 1) * L] for q in range(NQ)], axis=1)

    kk = k * kk_ref[...]
    kk = kk * lax.rsqrt(jnp.maximum(segsum(kk * kk), 1e-24))
    k = k * (1.0 + (a - 1.0) * ka_ref[...])
    bonus = segsum(r * k * rk_ref[...]) * v

    tri_b = _tri(L).astype(BF16)
    lw_hi = logw.astype(BF16)
    lw_lo = (logw - lw_hi.astype(F32)).astype(BF16)
    cs = _bdot(tri_b, lw_hi) + _bdot(tri_b, lw_lo)
    p_in = jnp.exp(cs)
    p_inv = jnp.exp(-cs)
    p_ex = jnp.exp(cs - logw)
    p_last = p_in[L - 1:L, :]

    rt = r * p_in
    at = -kk * p_ex
    kt = k * p_inv
    bt = kk * a * p_inv

    per_tile = LANES // D
    lane = lax.broadcasted_iota(jnp.int32, (1, LANES), 1) // D
    hmb = [(lane == h).astype(BF16) for h in range(per_tile)]
    rowi = lax.broadcasted_iota(jnp.int32, (L, QW), 0)
    coli = lax.broadcasted_iota(jnp.int32, (L, QW), 1) % D
    strict = coli < rowi
    lower = coli <= rowi
    eye = (coli == rowi).astype(F32)
    zero_tile = jnp.zeros((L, LANES), BF16)

    def head_rows(mb):
        blocks = []
        for h in range(HQ):
            t = h // per_tile
            keep = mb[:, t * LANES:(t + 1) * LANES] * hmb[h % per_tile]
            blocks.append(jnp.concatenate(
                [keep if c == t else zero_tile for c in range(QW // LANES)], axis=1))
        return blocks

    def bd(m):
        return jnp.concatenate(head_rows(m.astype(BF16)), axis=0)

    def cat2(x, y):
        return jnp.concatenate([x, y], axis=0)

    xs, zs, vs, a_ab, a_ak, a_rb, a_rk = [], [], [], [], [], [], []
    for sl in qs:
        xq = cat2(at[:, sl], rt[:, sl]).astype(BF16)
        btb = bt[:, sl].astype(BF16)
        ktb = kt[:, sl].astype(BF16)
        ycat = jnp.concatenate(head_rows(btb) + head_rows(ktb), axis=0)
        aall = lax.dot_general(xq, ycat, (((1,), (1,)), ((), ())),
                               preferred_element_type=F32)
        xs.append(xq)
        zs.append(cat2(btb, ktb))
        vs.append(v[:, sl])
        a_ab.append(jnp.where(strict, aall[0:L, 0:QW], 0.0))
        a_ak.append(jnp.where(strict, aall[0:L, QW:2 * QW], 0.0))
        a_rb.append(jnp.where(lower, aall[L:2 * L, 0:QW], 0.0))
        a_rk.append(jnp.where(lower, aall[L:2 * L, QW:2 * QW], 0.0))

    def bdb(mb):
        return jnp.concatenate(head_rows(mb), axis=0)

    tinv = [eye + m for m in a_ab]
    pwb = [m.astype(BF16) for m in a_ab]
    pwb = [_bdot(pwb[q], bdb(pwb[q])).astype(BF16) for q in range(NQ)]
    sx = [_bdot(xs[q], st_s[q].astype(BF16)) for q in range(NQ)]
    av = [_bdot(cat2(a_ak[q], a_rk[q]).astype(BF16), bd(vs[q])) for q in range(NQ)]
    for _ in range(L.bit_length() - 3):
        both = [_bdot(cat2(tinv[q].astype(BF16), pwb[q]), bdb(pwb[q])) for q in range(NQ)]
        tinv = [tinv[q] + both[q][0:L] for q in range(NQ)]
        pwb = [both[q][L:2 * L].astype(BF16) for q in range(NQ)]
    tinv = [tinv[q] + _bdot(tinv[q].astype(BF16), bdb(pwb[q])) for q in range(NQ)]

    us = [_bdot(tinv[q].astype(BF16), bd(sx[q][0:L] + av[q][0:L])) for q in range(NQ)]
    ys = [sx[q][L:2 * L] + av[q][L:2 * L] + _bdot(a_rb[q].astype(BF16), bd(us[q]))
          for q in range(NQ)]

    for q, sl in enumerate(qs):
        upd = lax.dot_general(zs[q], cat2(us[q], vs[q]).astype(BF16),
                              (((0,), (0,)), ((), ())),
                              preferred_element_type=F32)
        cols = []
        for cb in range(QW // LANES):
            lo_l = q * QW + cb * LANES
            cols.append(jnp.broadcast_to(p_last[:, lo_l:lo_l + LANES], (LANES, LANES)).T)
        pcol = jnp.concatenate(cols, axis=0)
        pcol = jnp.concatenate([pcol] * (QW // LANES), axis=1)
        st_s[q] = pcol * (st_s[q] + jnp.where(seg, upd, 0.0))

    yv = jnp.concatenate(ys, axis=1)
    cen = yv - segsum(yv) * (1.0 / D)
    var = segsum(cen * cen) * (1.0 / D)
    y = cen * lax.rsqrt(var + RWKV_GN_EPS) * gg_ref[...] + gb_ref[...] + bonus
    o_ref[...] = (y * _silu(z_ref[...].astype(F32))).astype(BF16)


def _l1_rwkv(u, lo, p, batch, seq, width):
    L = RWKV_CHUNK
    QW = RWKV_QUAD_WIDTH
    wt = RWKV_QUADS_PER_STEP * QW
    ngrp = width // wt
    nst = seq // L
    vec = lambda: pl.BlockSpec((1, wt), lambda b, g, s: (0, g))
    col = lambda off: pl.BlockSpec((L, wt), lambda b, g, s: (b * nst + s, off + g))
    return pl.pallas_call(
        _rwkv_kernel,
        grid=(batch, ngrp, nst),
        in_specs=[col(0), col(ngrp), col(2 * ngrp), col(3 * ngrp),
                  pl.BlockSpec((L, 2 * LORA_PAD), lambda b, g, s: (b * nst + s, 0)),
                  vec(),
                  pl.BlockSpec((LORA_PAD, wt), lambda b, g, s: (0, g)),
                  vec(),
                  pl.BlockSpec((LORA_PAD, wt), lambda b, g, s: (0, g)),
                  vec(), vec(), vec(), vec(), vec()],
        out_specs=pl.BlockSpec((L, wt), lambda b, g, s: (b * nst + s, g)),
        out_shape=jax.ShapeDtypeStruct((batch * seq, width), BF16),
        scratch_shapes=[pltpu.VMEM((RWKV_QUADS_PER_STEP, QW, QW), F32)],
        compiler_params=_params(("arbitrary", "arbitrary", "arbitrary")),
        name="l1_rwkv",
    )(u, u, u, u, lo, p["w0"], p["w2"], p["a0"], p["a2"],
      p["kk"], p["ka"], p["rk"], p["gg"], p["gb"])


def _bd4_dense(w):
    nb, bs, _ = w.shape
    per = MXU_DIM // bs
    rep = jnp.tile(w.reshape(nb // per, MXU_DIM, bs), (1, 1, per))
    idx = np.arange(MXU_DIM) // bs
    keep = jnp.asarray(idx[:, None] == idx[None, :])
    return jnp.where(keep, rep, 0.0).astype(BF16)


def _row(v):
    return v.reshape(1, -1).astype(F32)


def _pad_rows(w, rows):
    return jnp.pad(w, ((0, rows - w.shape[0]), (0, 0)))


def _pad_cols(w, cols):
    return jnp.pad(w, ((0, 0), (0, cols - w.shape[1])))


def kernel(x, l0_norm_pre, l0_w_in, l0_lru_conv_w, l0_lru_conv_b, l0_lru_wa, l0_lru_ba,
           l0_lru_wx, l0_lru_bx, l0_lru_lambda, l0_m_conv_w, l0_m_conv_b, l0_m_wq, l0_m_wk,
           l0_m_wv, l0_m_wi, l0_m_bi, l0_m_wf, l0_m_bf, l0_m_skip, l0_m_gn, l0_w_out,
           l0_norm_post, l1_norm_pre, l1_w_in, l1_mu_rkv, l1_mu_w, l1_mu_a, l1_w0, l1_w1,
           l1_w2, l1_a0, l1_a1, l1_a2, l1_k_k, l1_k_a, l1_r_k, l1_gn_g, l1_gn_b, l1_w_out,
           l1_norm_post):
    batch, seq, d = x.shape
    x2 = x.reshape(batch * seq, d).astype(F32)
    rw = l0_lru_lambda.shape[0]
    mw = l0_m_skip.shape[0]
    ww = l1_w0.shape[0]
    assert rw == mw, "column blocking assumes equal head-group widths"

    u0 = _l0_in(x2, _row(l0_norm_pre), l0_w_in.astype(BF16))

    lru_p = dict(cw=l0_lru_conv_w.astype(F32), cb=_row(l0_lru_conv_b),
                 wa=l0_lru_wa.astype(BF16), ba=_row(l0_lru_ba),
                 wx=l0_lru_wx.astype(BF16), bx=_row(l0_lru_bx), lam=_row(l0_lru_lambda))
    yr = _l0_lru(u0, lru_p, batch, seq, rw)

    wg = jnp.concatenate([l0_m_wi, l0_m_wf], axis=1)
    wg = _pad_cols(wg, LANES).reshape(3, mw, LANES).astype(BF16)
    bg = _pad_cols(jnp.concatenate([l0_m_bi, l0_m_bf]).reshape(1, -1), LANES).astype(F32)
    m_p = dict(cw=l0_m_conv_w.astype(F32), cb=_row(l0_m_conv_b),
               wq=_bd4_dense(l0_m_wq), wk=_bd4_dense(l0_m_wk),
               wv=_bd4_dense(l0_m_wv), wg=wg, bg=bg,
               skip=_row(l0_m_skip), gn=_row(l0_m_gn))
    ym = _l0_mlstm(u0, m_p, batch, seq, mw, x_col=2 * rw // mw, z_col=(2 * rw + mw) // mw)

    w_out0 = l0_w_out.astype(BF16).reshape(2, rw, d)
    x1 = _out_proj([yr, ym], w_out0, _row(l0_norm_post), x2)

    w1 = _pad_cols(l1_w1, LORA_PAD).astype(BF16)
    a1 = _pad_cols(l1_a1, LORA_PAD).astype(BF16)
    u1, lo1 = _l1_in(x1, _row(l1_norm_pre), l1_w_in.astype(BF16), _row(l1_mu_w),
                           _row(l1_mu_a), w1, a1, _row(l1_mu_rkv), seq)
    r_p = dict(w0=_row(l1_w0),
               w2=_pad_rows(l1_w2, LORA_PAD).astype(BF16), a0=_row(l1_a0),
               a2=_pad_rows(l1_a2, LORA_PAD).astype(BF16), kk=_row(l1_k_k), ka=_row(l1_k_a),
               rk=_row(l1_r_k), gg=_row(l1_gn_g), gb=_row(l1_gn_b))
    y1 = _l1_rwkv(u1, lo1, r_p, batch, seq, ww)
    out = _out_proj([y1], l1_w_out.astype(BF16).reshape(1, ww, d), _row(l1_norm_post), x1)
    return out.reshape(batch, seq, d).astype(x.dtype)
```

```python
import functools

import jax
import jax.numpy as jnp
import numpy as np
from jax import lax
from jax.experimental import pallas as pl
from jax.experimental.pallas import tpu as pltpu

F32 = jnp.float32
BF16 = jnp.bfloat16

RMS_EPS = 1e-6
CONV_WIDTH = 4
LRU_C = 8.0
LRU_BLOCK = 128
MLSTM_HEADS = 8
MLSTM_QKV_BLOCK = 4
MLSTM_CHUNK = 128
MLSTM_GN_EPS = 1e-6
RWKV_HEAD_DIM = 64
RWKV_GN_EPS = 64e-5
RWKV_CHUNK = 64
RWKV_DECAY_SCALE = 0.6065306597126334
RWKV_QUAD_WIDTH = 256
RWKV_QUADS_PER_STEP = 16
LORA_PAD = 128

LANES = 128
SUBLANES = 8
MXU_DIM = 256
VMEM_LIMIT = 56 * 1024 * 1024


def _params(sem):
    return pltpu.CompilerParams(dimension_semantics=sem, vmem_limit_bytes=VMEM_LIMIT)


def _sigmoid(x):
    return 1.0 / (1.0 + jnp.exp(-x))


def _silu(x):
    return x * _sigmoid(x)


def _softplus(x):
    return jnp.maximum(x, 0.0) + jnp.log1p(jnp.exp(-jnp.abs(x)))


def _dot(a, b):
    return jnp.dot(a.astype(BF16), b.astype(BF16), preferred_element_type=F32)


def _split3(x):
    hi = x.astype(BF16)
    r1 = x - hi.astype(F32)
    mid = r1.astype(BF16)
    lo = (r1 - mid.astype(F32)).astype(BF16)
    return hi, mid, lo


def _dot_exact_lhs(a_bf16, b_f32):
    hi, mid, lo = _split3(b_f32)
    out = jnp.dot(a_bf16, hi, preferred_element_type=F32)
    out = out + jnp.dot(a_bf16, mid, preferred_element_type=F32)
    return out + jnp.dot(a_bf16, lo, preferred_element_type=F32)


def _rms_norm(x, g):
    return x * lax.rsqrt(jnp.mean(x * x, axis=-1, keepdims=True) + RMS_EPS) * g


def _tri(n, inclusive=True):
    r = lax.broadcasted_iota(jnp.int32, (n, n), 0)
    c = lax.broadcasted_iota(jnp.int32, (n, n), 1)
    return (c <= r) if inclusive else (c < r)


def _l0_in_kernel(x_ref, g_ref, w_ref, o_ref, h_ref):
    @pl.when(pl.program_id(1) == 0)
    def _():
        h_ref[...] = _rms_norm(x_ref[...], g_ref[...]).astype(BF16)

    o_ref[...] = jnp.dot(h_ref[...], w_ref[...], preferred_element_type=F32)


def _l0_in(x2, g, w_bf16, tm=1024, tn=1024):
    t, d = x2.shape
    n = w_bf16.shape[1]
    return pl.pallas_call(
        _l0_in_kernel,
        grid=(t // tm, n // tn),
        in_specs=[pl.BlockSpec((tm, d), lambda i, j: (i, 0)),
                  pl.BlockSpec((1, d), lambda i, j: (0, 0)),
                  pl.BlockSpec((d, tn), lambda i, j: (0, j))],
        out_specs=pl.BlockSpec((tm, tn), lambda i, j: (i, j)),
        out_shape=jax.ShapeDtypeStruct((t, n), F32),
        scratch_shapes=[pltpu.VMEM((tm, d), BF16)],
        compiler_params=_params(("arbitrary", "arbitrary")),
        name="l0_in",
    )(x2, g, w_bf16)


def _causal_conv(x, tail_ref, w, b):
    ts = x.shape[0]
    row = lax.broadcasted_iota(jnp.int32, (SUBLANES, x.shape[1]), 0)
    tail = tail_ref[...]
    y = b + x * w[CONV_WIDTH - 1:CONV_WIDTH, :]
    for k in range(1, CONV_WIDTH):
        rolled = pltpu.roll(x, k, 0)
        top = jnp.where(row < k, pltpu.roll(tail, k, 0), rolled[0:SUBLANES])
        xk = jnp.concatenate([top, rolled[SUBLANES:]], axis=0)
        y = y + xk * w[CONV_WIDTH - 1 - k:CONV_WIDTH - k, :]
    tail_ref[...] = x[ts - SUBLANES:ts]
    return y


def _lru_kernel(x_ref, z_ref, cw_ref, cb_ref, wa_ref, ba_ref, wx_ref, bx_ref, lam_ref, o_ref,
                xbuf, a_p, u_p, hl_s, al_s, hc_s, *, ts, nblk, pitch):
    s = pl.program_id(2)

    @pl.when(s == 0)
    def _():
        xbuf[...] = jnp.zeros_like(xbuf)
        hc_s[...] = jnp.zeros_like(hc_s)

    xc = _causal_conv(x_ref[...], xbuf, cw_ref[...], cb_ref[...])

    gr, gi = [], []
    for n in range(nblk):
        xn = xc[:, n * LRU_BLOCK:(n + 1) * LRU_BLOCK]
        gr.append(_dot(xn, wa_ref[n]))
        gi.append(_dot(xn, wx_ref[n]))
    r = _sigmoid(jnp.concatenate(gr, axis=1) + ba_ref[...])
    i = _sigmoid(jnp.concatenate(gi, axis=1) + bx_ref[...])
    log_a = -LRU_C * r * _softplus(-lam_ref[...])
    a = jnp.exp(log_a)
    u = jnp.sqrt(-jnp.tanh(log_a) * (a * a + 1.0)) * (i * xc)

    seg = ts // SUBLANES
    nslab = a.shape[1] // LANES
    for c in range(nslab):
        for j in range(SUBLANES):
            a_p[c, pl.ds(j * pitch, seg), :] = a[j * seg:(j + 1) * seg, c * LANES:(c + 1) * LANES]
            u_p[c, pl.ds(j * pitch, seg), :] = u[j * seg:(j + 1) * seg, c * LANES:(c + 1) * LANES]

    def local_scan(t, carry):
        out = []
        for c in range(nslab):
            h, acc = carry[c]
            at = a_p[c, pl.ds(t, SUBLANES, stride=pitch), :]
            ut = u_p[c, pl.ds(t, SUBLANES, stride=pitch), :]
            h = at * h + ut
            acc = at * acc
            hl_s[c, t] = h
            al_s[c, t] = acc
            out.append((h, acc))
        return tuple(out)

    init = tuple((jnp.zeros((SUBLANES, LANES), F32), jnp.ones((SUBLANES, LANES), F32))
                 for _ in range(nslab))
    ends = lax.fori_loop(0, seg, local_scan, init, unroll=4)

    row = lax.broadcasted_iota(jnp.int32, (SUBLANES, LANES), 0)
    for c in range(nslab):
        hs, ac = ends[c]
        for sft in (1, 2, 4):
            a_sh = jnp.where(row >= sft, pltpu.roll(ac, sft, 0), 1.0)
            h_sh = jnp.where(row >= sft, pltpu.roll(hs, sft, 0), 0.0)
            hs = ac * h_sh + hs
            ac = ac * a_sh
        carry = hc_s[:, c * LANES:(c + 1) * LANES]
        h_end = ac * carry + hs
        h_in = jnp.where(row == 0, carry, pltpu.roll(h_end, 1, 0))
        hc_s[:, c * LANES:(c + 1) * LANES] = h_end[SUBLANES - 1:SUBLANES, :]

        def fix(t, _, c=c, h_in=h_in):
            u_p[c, pl.ds(t, SUBLANES, stride=pitch), :] = hl_s[c, t] + al_s[c, t] * h_in
            return 0

        lax.fori_loop(0, seg, fix, 0, unroll=4)

    for c in range(nslab):
        for j in range(SUBLANES):
            rows = slice(j * seg, (j + 1) * seg)
            lanes = slice(c * LANES, (c + 1) * LANES)
            hj = u_p[c, pl.ds(j * pitch, seg), :]
            o_ref[rows, lanes] = (hj * _silu(z_ref[rows, lanes].astype(F32))).astype(BF16)


def _l0_lru(u, p, batch, seq, width, ts=1024, tc=512):
    nblk = tc // LRU_BLOCK
    nct = width // tc
    nst = seq // ts
    seg = ts // SUBLANES
    pitch = seg + SUBLANES
    assert seg % (2 * SUBLANES) == 0
    nslab = tc // LANES
    zoff = width // tc
    vec = lambda: pl.BlockSpec((1, tc), lambda b, c, s: (0, c))
    return pl.pallas_call(
        functools.partial(_lru_kernel, ts=ts, nblk=nblk, pitch=pitch),
        grid=(batch, nct, nst),
        in_specs=[pl.BlockSpec((ts, tc), lambda b, c, s: (b * nst + s, c)),
                  pl.BlockSpec((ts, tc), lambda b, c, s: (b * nst + s, zoff + c)),
                  pl.BlockSpec((CONV_WIDTH, tc), lambda b, c, s: (0, c)),
                  vec(),
                  pl.BlockSpec((nblk, LRU_BLOCK, LRU_BLOCK), lambda b, c, s: (c, 0, 0)),
                  vec(),
                  pl.BlockSpec((nblk, LRU_BLOCK, LRU_BLOCK), lambda b, c, s: (c, 0, 0)),
                  vec(), vec()],
        out_specs=pl.BlockSpec((ts, tc), lambda b, c, s: (b * nst + s, c)),
        out_shape=jax.ShapeDtypeStruct((batch * seq, width), BF16),
        scratch_shapes=[pltpu.VMEM((SUBLANES, tc), F32),
                        pltpu.VMEM((nslab, SUBLANES * pitch, LANES), F32),
                        pltpu.VMEM((nslab, SUBLANES * pitch, LANES), F32),
                        pltpu.VMEM((nslab, seg, SUBLANES, LANES), F32),
                        pltpu.VMEM((nslab, seg, SUBLANES, LANES), F32),
                        pltpu.VMEM((1, tc), F32)],
        compiler_params=_params(("arbitrary", "arbitrary", "arbitrary")),
        name="l0_lru",
    )(u, u, p["cw"], p["cb"], p["wa"], p["ba"], p["wx"], p["bx"], p["lam"])


def _bd4(xb, w_ref):
    nb, bw, _ = w_ref.shape
    return jnp.concatenate(
        [jnp.dot(xb[:, c * bw:(c + 1) * bw], w_ref[c], preferred_element_type=F32)
         for c in range(nb)], axis=1)


def _mlstm_kernel(x_ref, z_ref, cw_ref, cb_ref, wq_ref, wk_ref, wv_ref, wg_ref, bg_ref,
                  skip_ref, gn_ref, o_ref, xbuf, c_s, n_s, m_s, *, hd):
    L = MLSTM_CHUNK
    H = MLSTM_HEADS
    s = pl.program_id(1)

    @pl.when(s == 0)
    def _():
        xbuf[...] = jnp.zeros_like(xbuf)
        c_s[...] = jnp.zeros_like(c_s)
        n_s[...] = jnp.zeros_like(n_s)
        m_s[...] = jnp.zeros_like(m_s)

    x = x_ref[...]
    xmc = _silu(_causal_conv(x, xbuf, cw_ref[...], cb_ref[...]))

    xmc_b = xmc.astype(BF16)
    q = _bd4(xmc_b, wq_ref)
    k = _bd4(xmc_b, wk_ref)
    v = _bd4(x.astype(BF16), wv_ref)

    g = _dot(q, wg_ref[0]) + _dot(k, wg_ref[1]) + _dot(v, wg_ref[2]) + bg_ref[...]
    lane = lax.broadcasted_iota(jnp.int32, (L, LANES), 1)
    lf = jnp.minimum(g, 0.0) - jnp.log1p(jnp.exp(-jnp.abs(g)))
    lf = jnp.where((lane >= H) & (lane < 2 * H), lf, 0.0)
    csum = _dot_exact_lhs(_tri(L).astype(BF16), lf)
    gb = jnp.where(lane < H, g, csum)
    gbt = gb.T

    causal = _tri(L)
    kscale = hd ** -0.5
    heads = range(H)
    sls = [slice(h * hd, (h + 1) * hd) for h in heads]
    qb = [q[:, sl].astype(BF16) for sl in sls]
    kh = [k[:, sl] * kscale for sl in sls]
    vb = [v[:, sl].astype(BF16) for sl in sls]
    s_raw = [lax.dot_general(qb[h], kh[h].astype(BF16), (((1,), (1,)), ((), ())),
                             preferred_element_type=F32) for h in heads]
    qc = [jnp.dot(qb[h], c_s[h].astype(BF16), preferred_element_type=F32) for h in heads]

    wgt, inter, m_t, decay, kw, m_new = [], [], [], [], [], []
    for h in heads:
        b_col = gb[:, H + h:H + h + 1]
        i_col = gb[:, h:h + 1]
        b_row = gbt[H + h:H + h + 1, :]
        i_row = gbt[h:h + 1, :]
        m_prev = m_s[h][0:1, 0:1]
        dm = jnp.where(causal, b_col - b_row + i_row, -1e30)
        m_inter = b_col + m_prev
        mt = jnp.maximum(jnp.max(dm, axis=-1, keepdims=True), m_inter)
        wgt.append(jnp.exp(dm - mt))
        inter.append(jnp.exp(m_inter - mt))
        m_t.append(mt)
        b_last = b_col[L - 1:L, :]
        g_row = b_last - b_row + i_row
        g_col = b_last - b_col + i_col
        mn = jnp.maximum(b_last + m_prev, jnp.max(g_row, axis=-1, keepdims=True))
        m_new.append(mn)
        decay.append(jnp.exp(b_last + m_prev - mn))
        kw.append(kh[h] * jnp.exp(g_col - mn))

    upd = [lax.dot_general(kw[h].astype(BF16), vb[h], (((0,), (0,)), ((), ())),
                           preferred_element_type=F32) for h in heads]
    scores = [s_raw[h] * wgt[h] for h in heads]
    num = [jnp.dot(scores[h].astype(BF16), vb[h], preferred_element_type=F32)
           + inter[h] * qc[h] for h in heads]

    ys = []
    for h in heads:
        nvec = n_s[h]
        den = (jnp.sum(scores[h], axis=-1, keepdims=True)
               + inter[h] * jnp.sum(q[:, sls[h]] * nvec, axis=-1, keepdims=True))
        hout = num[h] / jnp.maximum(jnp.abs(den), jnp.exp(-m_t[h]))
        c_s[h] = decay[h] * c_s[h] + upd[h]
        n_s[h] = decay[h] * nvec + jnp.sum(kw[h], axis=0, keepdims=True)
        m_s[h] = jnp.broadcast_to(m_new[h], (SUBLANES, LANES))
        mu = jnp.mean(hout, axis=-1, keepdims=True)
        cen = hout - mu
        var = jnp.mean(cen * cen, axis=-1, keepdims=True)
        ys.append(cen * lax.rsqrt(var + MLSTM_GN_EPS))
    y = jnp.concatenate(ys, axis=1) * gn_ref[...] + skip_ref[...] * xmc
    o_ref[...] = (y * _silu(z_ref[...].astype(F32))).astype(BF16)


def _l0_mlstm(u, p, batch, seq, width, x_col, z_col):
    L = MLSTM_CHUNK
    nst = seq // L
    hd = width // MLSTM_HEADS
    full = lambda shape: pl.BlockSpec(shape, lambda b, s: (0,) * len(shape))
    return pl.pallas_call(
        functools.partial(_mlstm_kernel, hd=hd),
        grid=(batch, nst),
        in_specs=[pl.BlockSpec((L, width), lambda b, s: (b * nst + s, x_col)),
                  pl.BlockSpec((L, width), lambda b, s: (b * nst + s, z_col)),
                  full((CONV_WIDTH, width)), full((1, width)),
                  full((width // MXU_DIM, MXU_DIM, MXU_DIM)),
                  full((width // MXU_DIM, MXU_DIM, MXU_DIM)),
                  full((width // MXU_DIM, MXU_DIM, MXU_DIM)),
                  full((3, width, LANES)), full((1, LANES)),
                  full((1, width)), full((1, width))],
        out_specs=pl.BlockSpec((L, width), lambda b, s: (b * nst + s, 0)),
        out_shape=jax.ShapeDtypeStruct((batch * seq, width), BF16),
        scratch_shapes=[pltpu.VMEM((SUBLANES, width), F32),
                        pltpu.VMEM((MLSTM_HEADS, hd, hd), F32),
                        pltpu.VMEM((MLSTM_HEADS, 1, hd), F32),
                        pltpu.VMEM((MLSTM_HEADS, SUBLANES, LANES), F32)],
        compiler_params=_params(("arbitrary", "arbitrary")),
        name="l0_mlstm",
    )(u, u, p["cw"], p["cb"], p["wq"], p["wk"], p["wv"], p["wg"], p["bg"], p["skip"], p["gn"])


def _out_kernel(*refs, nin):
    y_refs = refs[:nin]
    w_ref, g_ref, x_ref, o_ref = refs[nin:]
    acc = jnp.dot(y_refs[0][...], w_ref[0], preferred_element_type=F32)
    for n in range(1, nin):
        acc = acc + jnp.dot(y_refs[n][...], w_ref[n], preferred_element_type=F32)
    o_ref[...] = x_ref[...] + _rms_norm(acc, g_ref[...])


def _out_proj(ys, w_bf16, g, x2, tm=256):
    nin, kin, d = w_bf16.shape
    t = x2.shape[0]
    return pl.pallas_call(
        functools.partial(_out_kernel, nin=nin),
        grid=(t // tm,),
        in_specs=[pl.BlockSpec((tm, kin), lambda i: (i, 0)) for _ in range(nin)]
        + [pl.BlockSpec((nin, kin, d), lambda i: (0, 0, 0), pipeline_mode=pl.Buffered(1)),
           pl.BlockSpec((1, d), lambda i: (0, 0)),
           pl.BlockSpec((tm, d), lambda i: (i, 0))],
        out_specs=pl.BlockSpec((tm, d), lambda i: (i, 0)),
        out_shape=jax.ShapeDtypeStruct((t, d), F32),
        compiler_params=_params(("arbitrary",)),
        name="out_proj",
    )(*ys, w_bf16, g, x2)


def _shift_rows(x, last_ref, first):
    row = lax.broadcasted_iota(jnp.int32, x.shape, 0)
    last = jnp.where(first, 0.0, last_ref[...])
    prev = jnp.where(row == 0, last, pltpu.roll(x, 1, 0))
    last_ref[...] = x[x.shape[0] - 1:x.shape[0], :]
    return prev


def _l1_in_kernel(x_ref, g_ref, w_ref, muw_ref, mua_ref, w1_ref, a1_ref, mu_ref, o_ref,
                  lo_ref, h_ref, last_ref, lastu_ref, *, tiles_per_seq):
    i = pl.program_id(0)
    j = pl.program_id(1)
    first = i % tiles_per_seq == 0

    @pl.when(j == 0)
    def _():
        h = _rms_norm(x_ref[...], g_ref[...])
        h_ref[...] = h.astype(BF16)
        dh = _shift_rows(h, last_ref, first) - h
        xw = h + dh * muw_ref[...]
        xa = h + dh * mua_ref[...]
        lo_ref[:, 0:LORA_PAD] = jnp.tanh(_dot(xw, w1_ref[...]))
        lo_ref[:, LORA_PAD:2 * LORA_PAD] = _dot(xa, a1_ref[...])

    acc = jnp.dot(h_ref[...], w_ref[...], preferred_element_type=F32)
    o_ref[...] = acc + (_shift_rows(acc, lastu_ref.at[j], first) - acc) * mu_ref[...]


def _l1_in(x2, g, w_bf16, muw, mua, w1, a1, mu_rkv, seq, tm=1024, tn=1024):
    t, d = x2.shape
    n = w_bf16.shape[1]
    mu_rkv = _pad_cols(mu_rkv, n)
    c2 = lambda shape: pl.BlockSpec(shape, lambda i, j: (0, 0))
    return pl.pallas_call(
        functools.partial(_l1_in_kernel, tiles_per_seq=seq // tm),
        grid=(t // tm, n // tn),
        in_specs=[pl.BlockSpec((tm, d), lambda i, j: (i, 0)),
                  c2((1, d)),
                  pl.BlockSpec((d, tn), lambda i, j: (0, j)),
                  c2((1, d)), c2((1, d)), c2((d, LORA_PAD)), c2((d, LORA_PAD)),
                  pl.BlockSpec((1, tn), lambda i, j: (0, j))],
        out_specs=[pl.BlockSpec((tm, tn), lambda i, j: (i, j)),
                   pl.BlockSpec((tm, 2 * LORA_PAD), lambda i, j: (i, 0))],
        out_shape=[jax.ShapeDtypeStruct((t, n), F32),
                   jax.ShapeDtypeStruct((t, 2 * LORA_PAD), F32)],
        scratch_shapes=[pltpu.VMEM((tm, d), BF16), pltpu.VMEM((1, d), F32),
                        pltpu.VMEM((n // tn, 1, tn), F32)],
        compiler_params=_params(("arbitrary", "arbitrary")),
        name="l1_in",
    )(x2, g, w_bf16, muw, mua, w1, a1, mu_rkv)


def _bdot(a, b):
    return jnp.dot(a, b, preferred_element_type=F32)


def _rwkv_kernel(r_ref, k_ref, v_ref, z_ref, lo_ref, w0_ref, w2_ref,
                 a0_ref, a2_ref, kk_ref, ka_ref, rk_ref, gg_ref, gb_ref, o_ref, st_s):
    L = RWKV_CHUNK
    D = RWKV_HEAD_DIM
    QW = RWKV_QUAD_WIDTH
    HQ = QW // D
    NQ = r_ref.shape[1] // QW
    s = pl.program_id(2)

    @pl.when(s == 0)
    def _():
        st_s[...] = jnp.zeros_like(st_s)

    r = r_ref[...]
    k = k_ref[...]
    v = v_ref[...]

    lo = lo_ref[...]
    logw = -RWKV_DECAY_SCALE * _sigmoid(w0_ref[...] + _dot(lo[:, 0:LORA_PAD], w2_ref[...]))
    a = _sigmoid(a0_ref[...] + _dot(lo[:, LORA_PAD:2 * LORA_PAD], a2_ref[...]))

    seg = ((lax.broadcasted_iota(jnp.int32, (QW, QW), 0) // D)
           == (lax.broadcasted_iota(jnp.int32, (QW, QW), 1) // D))
    seg_b = seg.astype(BF16)
    qs = [slice(q * QW, (q + 1) * QW) for q in range(NQ)]

    def segsum(x):
        stacked = jnp.concatenate([x[:, sl] for sl in qs], axis=0).astype(BF16)
        res = _bdot(stacked, seg_b)
        return jnp.concatenate([res[q * L:(q + 1) * L] for q in range(NQ)], axis=1)

    kk = k * kk_ref[...]
    kk = kk * lax.rsqrt(jnp.maximum(segsum(kk * kk), 1e-24))
    k = k * (1.0 + (a - 1.0) * ka_ref[...])
    bonus = segsum(r * k * rk_ref[...]) * v

    tri_b = _tri(L).astype(BF16)
    lw_hi = logw.astype(BF16)
    lw_lo = (logw - lw_hi.astype(F32)).astype(BF16)
    cs = _bdot(tri_b, lw_hi) + _bdot(tri_b, lw_lo)
    p_in = jnp.exp(cs)
    p_inv = jnp.exp(-cs)
    p_ex = jnp.exp(cs - logw)
    p_last = p_in[L - 1:L, :]

    rt = r * p_in
    at = -kk * p_ex
    kt = k * p_inv
    bt = kk * a * p_inv

    per_tile = LANES // D
    lane = lax.broadcasted_iota(jnp.int32, (1, LANES), 1) // D
    hmb = [(lane == h).astype(BF16) for h in range(per_tile)]
    rowi = lax.broadcasted_iota(jnp.int32, (L, QW), 0)
    coli = lax.broadcasted_iota(jnp.int32, (L, QW), 1) % D
    strict = coli < rowi
    lower = coli <= rowi
    eye = (coli == rowi).astype(F32)
    zero_tile = jnp.zeros((L, LANES), BF16)

    def head_rows(mb):
        blocks = []
        for h in range(HQ):
            t = h // per_tile
            keep = mb[:, t * LANES:(t + 1) * LANES] * hmb[h % per_tile]
            blocks.append(jnp.concatenate(
                [keep if c == t else zero_tile for c in range(QW // LANES)], axis=1))
        return blocks

    def bd(m):
        return jnp.concatenate(head_rows(m.astype(BF16)), axis=0)

    def cat2(x, y):
        return jnp.concatenate([x, y], axis=0)

    xs, zs, vs, a_ab, a_ak, a_rb, a_rk = [], [], [], [], [], [], []
    for sl in qs:
        xq = cat2(at[:, sl], rt[:, sl]).astype(BF16)
        btb = bt[:, sl].astype(BF16)
        ktb = kt[:, sl].astype(BF16)
        ycat = jnp.concatenate(head_rows(btb) + head_rows(ktb), axis=0)
        aall = lax.dot_general(xq, ycat, (((1,), (1,)), ((), ())),
                               preferred_element_type=F32)
        xs.append(xq)
        zs.append(cat2(btb, ktb))
        vs.append(v[:, sl])
        a_ab.append(jnp.where(strict, aall[0:L, 0:QW], 0.0))
        a_ak.append(jnp.where(strict, aall[0:L, QW:2 * QW], 0.0))
        a_rb.append(jnp.where(lower, aall[L:2 * L, 0:QW], 0.0))
        a_rk.append(jnp.where(lower, aall[L:2 * L, QW:2 * QW], 0.0))

    def bdb(mb):
        return jnp.concatenate(head_rows(mb), axis=0)

    tinv = [eye + m for m in a_ab]
    pwb = [m.astype(BF16) for m in a_ab]
    pwb = [_bdot(pwb[q], bdb(pwb[q])).astype(BF16) for q in range(NQ)]
    sx = [_bdot(xs[q], st_s[q].astype(BF16)) for q in range(NQ)]
    av = [_bdot(cat2(a_ak[q], a_rk[q]).astype(BF16), bd(vs[q])) for q in range(NQ)]
    for _ in range(L.bit_length() - 3):
        both = [_bdot(cat2(tinv[q].astype(BF16), pwb[q]), bdb(pwb[q])) for q in range(NQ)]
        tinv = [tinv[q] + both[q][0:L] for q in range(NQ)]
        pwb = [both[q][L:2 * L].astype(BF16) for q in range(NQ)]
    tinv = [tinv[q] + _bdot(tinv[q].astype(BF16), bdb(pwb[q])) for q in range(NQ)]

    us = [_bdot(tinv[q].astype(BF16), bd(sx[q][0:L] + av[q][0:L])) for q in range(NQ)]
    ys = [sx[q][L:2 * L] + av[q][L:2 * L] + _bdot(a_rb[q].astype(BF16), bd(us[q]))
          for q in range(NQ)]

    for q, sl in enumerate(qs):
        upd = lax.dot_general(zs[q], cat2(us[q], vs[q]).astype(BF16),
                              (((0,), (0,)), ((), ())),
                              preferred_element_type=F32)
        cols = []
        for cb in range(QW // LANES):
            lo_l = q * QW + cb * LANES
            cols.append(jnp.broadcast_to(p_last[:, lo_l:lo_l + LANES], (LANES, LANES)).T)
        pcol = jnp.concatenate(cols, axis=0)
        pcol = jnp.concatenate([pcol] * (QW // LANES), axis=1)
        st_s[q] = pcol * (st_s[q] + jnp.where(seg, upd, 0.0))

    yv = jnp.concatenate(ys, axis=1)
    cen = yv - segsum(yv) * (1.0 / D)
    var = segsum(cen * cen) * (1.0 / D)
    y = cen * lax.rsqrt(var + RWKV_GN_EPS) * gg_ref[...] + gb_ref[...] + bonus
    o_ref[...] = (y * _silu(z_ref[...].astype(F32))).astype(BF16)


def _l1_rwkv(u, lo, p, batch, seq, width):
    L = RWKV_CHUNK
    QW = RWKV_QUAD_WIDTH
    wt = RWKV_QUADS_PER_STEP * QW
    ngrp = width // wt
    nst = seq // L
    vec = lambda: pl.BlockSpec((1, wt), lambda b, g, s: (0, g))
    col = lambda off: pl.BlockSpec((L, wt), lambda b, g, s: (b * nst + s, off + g))
    return pl.pallas_call(
        _rwkv_kernel,
        grid=(batch, ngrp, nst),
        in_specs=[col(0), col(ngrp), col(2 * ngrp), col(3 * ngrp),
                  pl.BlockSpec((L, 2 * LORA_PAD), lambda b, g, s: (b * nst + s, 0)),
                  vec(),
                  pl.BlockSpec((LORA_PAD, wt), lambda b, g, s: (0, g)),
                  vec(),
                  pl.BlockSpec((LORA_PAD, wt), lambda b, g, s: (0, g)),
                  vec(), vec(), vec(), vec(), vec()],
        out_specs=pl.BlockSpec((L, wt), lambda b, g, s: (b * nst + s, g)),
        out_shape=jax.ShapeDtypeStruct((batch * seq, width), BF16),
        scratch_shapes=[pltpu.VMEM((RWKV_QUADS_PER_STEP, QW, QW), F32)],
        compiler_params=_params(("arbitrary", "arbitrary", "arbitrary")),
        name="l1_rwkv",
    )(u, u, u, u, lo, p["w0"], p["w2"], p["a0"], p["a2"],
      p["kk"], p["ka"], p["rk"], p["gg"], p["gb"])


def _bd4_dense(w):
    nb, bs, _ = w.shape
    per = MXU_DIM // bs
    rep = jnp.tile(w.reshape(nb // per, MXU_DIM, bs), (1, 1, per))
    idx = np.arange(MXU_DIM) // bs
    keep = jnp.asarray(idx[:, None] == idx[None, :])
    return jnp.where(keep, rep, 0.0).astype(BF16)


def _row(v):
    return v.reshape(1, -1).astype(F32)


def _pad_rows(w, rows):
    return jnp.pad(w, ((0, rows - w.shape[0]), (0, 0)))


def _pad_cols(w, cols):
    return jnp.pad(w, ((0, 0), (0, cols - w.shape[1])))


def kernel(x, l0_norm_pre, l0_w_in, l0_lru_conv_w, l0_lru_conv_b, l0_lru_wa, l0_lru_ba,
           l0_lru_wx, l0_lru_bx, l0_lru_lambda, l0_m_conv_w, l0_m_conv_b, l0_m_wq, l0_m_wk,
           l0_m_wv, l0_m_wi, l0_m_bi, l0_m_wf, l0_m_bf, l0_m_skip, l0_m_gn, l0_w_out,
           l0_norm_post, l1_norm_pre, l1_w_in, l1_mu_rkv, l1_mu_w, l1_mu_a, l1_w0, l1_w1,
           l1_w2, l1_a0, l1_a1, l1_a2, l1_k_k, l1_k_a, l1_r_k, l1_gn_g, l1_gn_b, l1_w_out,
           l1_norm_post):
    batch, seq, d = x.shape
    x2 = x.reshape(batch * seq, d).astype(F32)
    rw = l0_lru_lambda.shape[0]
    mw = l0_m_skip.shape[0]
    ww = l1_w0.shape[0]
    assert rw == mw, "column blocking assumes equal head-group widths"

    u0 = _l0_in(x2, _row(l0_norm_pre), l0_w_in.astype(BF16))

    lru_p = dict(cw=l0_lru_conv_w.astype(F32), cb=_row(l0_lru_conv_b),
                 wa=l0_lru_wa.astype(BF16), ba=_row(l0_lru_ba),
                 wx=l0_lru_wx.astype(BF16), bx=_row(l0_lru_bx), lam=_row(l0_lru_lambda))
    yr = _l0_lru(u0, lru_p, batch, seq, rw)

    wg = jnp.concatenate([l0_m_wi, l0_m_wf], axis=1)
    wg = _pad_cols(wg, LANES).reshape(3, mw, LANES).astype(BF16)
    bg = _pad_cols(jnp.concatenate([l0_m_bi, l0_m_bf]).reshape(1, -1), LANES).astype(F32)
    m_p = dict(cw=l0_m_conv_w.astype(F32), cb=_row(l0_m_conv_b),
               wq=_bd4_dense(l0_m_wq), wk=_bd4_dense(l0_m_wk),
               wv=_bd4_dense(l0_m_wv), wg=wg, bg=bg,
               skip=_row(l0_m_skip), gn=_row(l0_m_gn))
    ym = _l0_mlstm(u0, m_p, batch, seq, mw, x_col=2 * rw // mw, z_col=(2 * rw + mw) // mw)

    w_out0 = l0_w_out.astype(BF16).reshape(2, rw, d)
    x1 = _out_proj([yr, ym], w_out0, _row(l0_norm_post), x2)

    w1 = _pad_cols(l1_w1, LORA_PAD).astype(BF16)
    a1 = _pad_cols(l1_a1, LORA_PAD).astype(BF16)
    u1, lo1 = _l1_in(x1, _row(l1_norm_pre), l1_w_in.astype(BF16), _row(l1_mu_w),
                           _row(l1_mu_a), w1, a1, _row(l1_mu_rkv), seq)
    r_p = dict(w0=_row(l1_w0),
               w2=_pad_rows(l1_w2, LORA_PAD).astype(BF16), a0=_row(l1_a0),
               a2=_pad_rows(l1_a2, LORA_PAD).astype(BF16), kk=_row(l1_k_k), ka=_row(l1_k_a),
               rk=_row(l1_r_k), gg=_row(l1_gn_g), gb=_row(l1_gn_b))
    y1 = _l1_rwkv(u1, lo1, r_p, batch, seq, ww)
    out = _out_proj([y1], l1_w_out.astype(BF16).reshape(1, ww, d), _row(l1_norm_post), x1)
    return out.reshape(batch, seq, d).astype(x.dtype)
```

```python
import functools

import jax
import jax.numpy as jnp
import numpy as np
from jax import lax
from jax.experimental import pallas as pl
from jax.experimental.pallas import tpu as pltpu

F32 = jnp.float32
BF16 = jnp.bfloat16

RMS_EPS = 1e-6
CONV_WIDTH = 4
LRU_C = 8.0
LRU_BLOCK = 128
MLSTM_HEADS = 8
MLSTM_QKV_BLOCK = 4
MLSTM_CHUNK = 128
MLSTM_GN_EPS = 1e-6
RWKV_HEAD_DIM = 64
RWKV_GN_EPS = 64e-5
RWKV_CHUNK = 64
RWKV_DECAY_SCALE = 0.6065306597126334
RWKV_QUAD_WIDTH = 256
RWKV_QUADS_PER_STEP = 16
LORA_PAD = 128

LANES = 128
SUBLANES = 8
MXU_DIM = 256
VMEM_LIMIT = 56 * 1024 * 1024


def _params(sem):
    return pltpu.CompilerParams(dimension_semantics=sem, vmem_limit_bytes=VMEM_LIMIT)


def _sigmoid(x):
    return 1.0 / (1.0 + jnp.exp(-x))


def _silu(x):
    return x * _sigmoid(x)


def _softplus(x):
    return jnp.maximum(x, 0.0) + jnp.log1p(jnp.exp(-jnp.abs(x)))


def _dot(a, b):
    return jnp.dot(a.astype(BF16), b.astype(BF16), preferred_element_type=F32)


def _split3(x):
    hi = x.astype(BF16)
    r1 = x - hi.astype(F32)
    mid = r1.astype(BF16)
    lo = (r1 - mid.astype(F32)).astype(BF16)
    return hi, mid, lo


def _dot_exact_lhs(a_bf16, b_f32):
    hi, mid, lo = _split3(b_f32)
    out = jnp.dot(a_bf16, hi, preferred_element_type=F32)
    out = out + jnp.dot(a_bf16, mid, preferred_element_type=F32)
    return out + jnp.dot(a_bf16, lo, preferred_element_type=F32)


def _rms_norm(x, g):
    return x * lax.rsqrt(jnp.mean(x * x, axis=-1, keepdims=True) + RMS_EPS) * g


def _tri(n, inclusive=True):
    r = lax.broadcasted_iota(jnp.int32, (n, n), 0)
    c = lax.broadcasted_iota(jnp.int32, (n, n), 1)
    return (c <= r) if inclusive else (c < r)


def _l0_in_kernel(x_ref, g_ref, w_ref, o_ref, h_ref):
    @pl.when(pl.program_id(1) == 0)
    def _():
        h_ref[...] = _rms_norm(x_ref[...], g_ref[...]).astype(BF16)

    o_ref[...] = jnp.dot(h_ref[...], w_ref[...], preferred_element_type=F32)


def _l0_in(x2, g, w_bf16, tm=1024, tn=1024):
    t, d = x2.shape
    n = w_bf16.shape[1]
    return pl.pallas_call(
        _l0_in_kernel,
        grid=(t // tm, n // tn),
        in_specs=[pl.BlockSpec((tm, d), lambda i, j: (i, 0)),
                  pl.BlockSpec((1, d), lambda i, j: (0, 0)),
                  pl.BlockSpec((d, tn), lambda i, j: (0, j))],
        out_specs=pl.BlockSpec((tm, tn), lambda i, j: (i, j)),
        out_shape=jax.ShapeDtypeStruct((t, n), F32),
        scratch_shapes=[pltpu.VMEM((tm, d), BF16)],
        compiler_params=_params(("arbitrary", "arbitrary")),
        name="l0_in",
    )(x2, g, w_bf16)


def _causal_conv(x, tail_ref, w, b):
    ts = x.shape[0]
    row = lax.broadcasted_iota(jnp.int32, (SUBLANES, x.shape[1]), 0)
    tail = tail_ref[...]
    y = b + x * w[CONV_WIDTH - 1:CONV_WIDTH, :]
    for k in range(1, CONV_WIDTH):
        rolled = pltpu.roll(x, k, 0)
        top = jnp.where(row < k, pltpu.roll(tail, k, 0), rolled[0:SUBLANES])
        xk = jnp.concatenate([top, rolled[SUBLANES:]], axis=0)
        y = y + xk * w[CONV_WIDTH - 1 - k:CONV_WIDTH - k, :]
    tail_ref[...] = x[ts - SUBLANES:ts]
    return y


def _lru_kernel(x_ref, z_ref, cw_ref, cb_ref, wa_ref, ba_ref, wx_ref, bx_ref, lam_ref, o_ref,
                xbuf, a_p, u_p, hl_s, al_s, hc_s, *, ts, nblk, pitch):
    s = pl.program_id(2)

    @pl.when(s == 0)
    def _():
        xbuf[...] = jnp.zeros_like(xbuf)
        hc_s[...] = jnp.zeros_like(hc_s)

    xc = _causal_conv(x_ref[...], xbuf, cw_ref[...], cb_ref[...])

    gr, gi = [], []
    for n in range(nblk):
        xn = xc[:, n * LRU_BLOCK:(n + 1) * LRU_BLOCK]
        gr.append(_dot(xn, wa_ref[n]))
        gi.append(_dot(xn, wx_ref[n]))
    r = _sigmoid(jnp.concatenate(gr, axis=1) + ba_ref[...])
    i = _sigmoid(jnp.concatenate(gi, axis=1) + bx_ref[...])
    log_a = -LRU_C * r * _softplus(-lam_ref[...])
    a = jnp.exp(log_a)
    u = jnp.sqrt(-jnp.tanh(log_a) * (a * a + 1.0)) * (i * xc)

    seg = ts // SUBLANES
    nslab = a.shape[1] // LANES
    for c in range(nslab):
        for j in range(SUBLANES):
            a_p[c, pl.ds(j * pitch, seg), :] = a[j * seg:(j + 1) * seg, c * LANES:(c + 1) * LANES]
            u_p[c, pl.ds(j * pitch, seg), :] = u[j * seg:(j + 1) * seg, c * LANES:(c + 1) * LANES]

    def local_scan(t, carry):
        out = []
        for c in range(nslab):
            h, acc = carry[c]
            at = a_p[c, pl.ds(t, SUBLANES, stride=pitch), :]
            ut = u_p[c, pl.ds(t, SUBLANES, stride=pitch), :]
            h = at * h + ut
            acc = at * acc
            hl_s[c, t] = h
            al_s[c, t] = acc
            out.append((h, acc))
        return tuple(out)

    init = tuple((jnp.zeros((SUBLANES, LANES), F32), jnp.ones((SUBLANES, LANES), F32))
                 for _ in range(nslab))
    ends = lax.fori_loop(0, seg, local_scan, init, unroll=4)

    row = lax.broadcasted_iota(jnp.int32, (SUBLANES, LANES), 0)
    for c in range(nslab):
        hs, ac = ends[c]
        for sft in (1, 2, 4):
            a_sh = jnp.where(row >= sft, pltpu.roll(ac, sft, 0), 1.0)
            h_sh = jnp.where(row >= sft, pltpu.roll(hs, sft, 0), 0.0)
            hs = ac * h_sh + hs
            ac = ac * a_sh
        carry = hc_s[:, c * LANES:(c + 1) * LANES]
        h_end = ac * carry + hs
        h_in = jnp.where(row == 0, carry, pltpu.roll(h_end, 1, 0))
        hc_s[:, c * LANES:(c + 1) * LANES] = h_end[SUBLANES - 1:SUBLANES, :]

        def fix(t, _, c=c, h_in=h_in):
            u_p[c, pl.ds(t, SUBLANES, stride=pitch), :] = hl_s[c, t] + al_s[c, t] * h_in
            return 0

        lax.fori_loop(0, seg, fix, 0, unroll=4)

    for c in range(nslab):
        for j in range(SUBLANES):
            rows = slice(j * seg, (j + 1) * seg)
            lanes = slice(c * LANES, (c + 1) * LANES)
            hj = u_p[c, pl.ds(j * pitch, seg), :]
            o_ref[rows, lanes] = (hj * _silu(z_ref[rows, lanes].astype(F32))).astype(BF16)


def _l0_lru(u, p, batch, seq, width, ts=1024, tc=512):
    nblk = tc // LRU_BLOCK
    nct = width // tc
    nst = seq // ts
    seg = ts // SUBLANES
    pitch = seg + SUBLANES
    assert seg % (2 * SUBLANES) == 0
    nslab = tc // LANES
    zoff = width // tc
    vec = lambda: pl.BlockSpec((1, tc), lambda b, c, s: (0, c))
    return pl.pallas_call(
        functools.partial(_lru_kernel, ts=ts, nblk=nblk, pitch=pitch),
        grid=(batch, nct, nst),
        in_specs=[pl.BlockSpec((ts, tc), lambda b, c, s: (b * nst + s, c)),
                  pl.BlockSpec((ts, tc), lambda b, c, s: (b * nst + s, zoff + c)),
                  pl.BlockSpec((CONV_WIDTH, tc), lambda b, c, s: (0, c)),
                  vec(),
                  pl.BlockSpec((nblk, LRU_BLOCK, LRU_BLOCK), lambda b, c, s: (c, 0, 0)),
                  vec(),
                  pl.BlockSpec((nblk, LRU_BLOCK, LRU_BLOCK), lambda b, c, s: (c, 0, 0)),
                  vec(), vec()],
        out_specs=pl.BlockSpec((ts, tc), lambda b, c, s: (b * nst + s, c)),
        out_shape=jax.ShapeDtypeStruct((batch * seq, width), BF16),
        scratch_shapes=[pltpu.VMEM((SUBLANES, tc), F32),
                        pltpu.VMEM((nslab, SUBLANES * pitch, LANES), F32),
                        pltpu.VMEM((nslab, SUBLANES * pitch, LANES), F32),
                        pltpu.VMEM((nslab, seg, SUBLANES, LANES), F32),
                        pltpu.VMEM((nslab, seg, SUBLANES, LANES), F32),
                        pltpu.VMEM((1, tc), F32)],
        compiler_params=_params(("arbitrary", "arbitrary", "arbitrary")),
        name="l0_lru",
    )(u, u, p["cw"], p["cb"], p["wa"], p["ba"], p["wx"], p["bx"], p["lam"])


def _bd4(xb, w_ref):
    nb, bw, _ = w_ref.shape
    return jnp.concatenate(
        [jnp.dot(xb[:, c * bw:(c + 1) * bw], w_ref[c], preferred_element_type=F32)
         for c in range(nb)], axis=1)


def _mlstm_kernel(x_ref, z_ref, cw_ref, cb_ref, wq_ref, wk_ref, wv_ref, wg_ref, bg_ref,
                  skip_ref, gn_ref, o_ref, xbuf, c_s, n_s, m_s, *, hd):
    L = MLSTM_CHUNK
    H = MLSTM_HEADS
    NB = x_ref.shape[0]
    s = pl.program_id(1)

    @pl.when(s == 0)
    def _():
        xbuf[...] = jnp.zeros_like(xbuf)
        c_s[...] = jnp.zeros_like(c_s)
        n_s[...] = jnp.zeros_like(n_s)
        m_s[...] = jnp.zeros_like(m_s)

    lane = lax.broadcasted_iota(jnp.int32, (L, LANES), 1)
    tri_b = _tri(L).astype(BF16)
    q, k, v, xmc, gb, gbt = [], [], [], [], [], []
    for b in range(NB):
        x = x_ref[b]
        xm = _silu(_causal_conv(x, xbuf.at[b], cw_ref[...], cb_ref[...]))
        xm_b = xm.astype(BF16)
        qb_ = _bd4(xm_b, wq_ref)
        kb_ = _bd4(xm_b, wk_ref)
        vb_ = _bd4(x.astype(BF16), wv_ref)
        g = _dot(qb_, wg_ref[0]) + _dot(kb_, wg_ref[1]) + _dot(vb_, wg_ref[2]) + bg_ref[...]
        lf = jnp.minimum(g, 0.0) - jnp.log1p(jnp.exp(-jnp.abs(g)))
        lf = jnp.where((lane >= H) & (lane < 2 * H), lf, 0.0)
        csum = _dot_exact_lhs(tri_b, lf)
        gbb = jnp.where(lane < H, g, csum)
        q.append(qb_); k.append(kb_); v.append(vb_); xmc.append(xm)
        gb.append(gbb); gbt.append(gbb.T)

    causal = _tri(L)
    kscale = hd ** -0.5
    pairs = [(b, h) for b in range(NB) for h in range(H)]
    np_ = len(pairs)
    sl = lambda h: slice(h * hd, (h + 1) * hd)
    qb = [q[b][:, sl(h)].astype(BF16) for b, h in pairs]
    kh = [k[b][:, sl(h)] * kscale for b, h in pairs]
    vb = [v[b][:, sl(h)].astype(BF16) for b, h in pairs]
    s_raw = [lax.dot_general(qb[i], kh[i].astype(BF16), (((1,), (1,)), ((), ())),
                             preferred_element_type=F32) for i in range(np_)]
    qc = [jnp.dot(qb[i], c_s[i].astype(BF16), preferred_element_type=F32) for i in range(np_)]

    wgt, inter, m_t, decay, kw, m_new = [], [], [], [], [], []
    for i, (b, h) in enumerate(pairs):
        b_col = gb[b][:, H + h:H + h + 1]
        i_col = gb[b][:, h:h + 1]
        b_row = gbt[b][H + h:H + h + 1, :]
        i_row = gbt[b][h:h + 1, :]
        m_prev = m_s[i][0:1, 0:1]
        dm = jnp.where(causal, b_col - b_row + i_row, -1e30)
        m_inter = b_col + m_prev
        mt = jnp.maximum(jnp.max(dm, axis=-1, keepdims=True), m_inter)
        wgt.append(jnp.exp(dm - mt))
        inter.append(jnp.exp(m_inter - mt))
        m_t.append(mt)
        b_last = b_col[L - 1:L, :]
        g_row = b_last - b_row + i_row
        g_col = b_last - b_col + i_col
        mn = jnp.maximum(b_last + m_prev, jnp.max(g_row, axis=-1, keepdims=True))
        m_new.append(mn)
        decay.append(jnp.exp(b_last + m_prev - mn))
        kw.append(kh[i] * jnp.exp(g_col - mn))

    upd = [lax.dot_general(kw[i].astype(BF16), vb[i], (((0,), (0,)), ((), ())),
                           preferred_element_type=F32) for i in range(np_)]
    scores = [s_raw[i] * wgt[i] for i in range(np_)]
    num = [jnp.dot(scores[i].astype(BF16), vb[i], preferred_element_type=F32)
           + inter[i] * qc[i] for i in range(np_)]

    ys = [[] for _ in range(NB)]
    for i, (b, h) in enumerate(pairs):
        nvec = n_s[i]
        den = (jnp.sum(scores[i], axis=-1, keepdims=True)
               + inter[i] * jnp.sum(q[b][:, sl(h)] * nvec, axis=-1, keepdims=True))
        hout = num[i] / jnp.maximum(jnp.abs(den), jnp.exp(-m_t[i]))
        c_s[i] = decay[i] * c_s[i] + upd[i]
        n_s[i] = decay[i] * nvec + jnp.sum(kw[i], axis=0, keepdims=True)
        m_s[i] = jnp.broadcast_to(m_new[i], (SUBLANES, LANES))
        mu = jnp.mean(hout, axis=-1, keepdims=True)
        cen = hout - mu
        var = jnp.mean(cen * cen, axis=-1, keepdims=True)
        ys[b].append(cen * lax.rsqrt(var + MLSTM_GN_EPS))
    for b in range(NB):
        y = jnp.concatenate(ys[b], axis=1) * gn_ref[...] + skip_ref[...] * xmc[b]
        o_ref[b] = (y * _silu(z_ref[b].astype(F32))).astype(BF16)


def _l0_mlstm(u, p, batch, seq, width, x_col, z_col, nb=2):
    L = MLSTM_CHUNK
    nst = seq // L
    hd = width // MLSTM_HEADS
    u3 = u.reshape(batch, seq, u.shape[1])
    full = lambda shape: pl.BlockSpec(shape, lambda b, s: (0,) * len(shape))
    out = pl.pallas_call(
        functools.partial(_mlstm_kernel, hd=hd),
        grid=(batch // nb, nst),
        in_specs=[pl.BlockSpec((nb, L, width), lambda b, s: (b, s, x_col)),
                  pl.BlockSpec((nb, L, width), lambda b, s: (b, s, z_col)),
                  full((CONV_WIDTH, width)), full((1, width)),
                  full((width // MXU_DIM, MXU_DIM, MXU_DIM)),
                  full((width // MXU_DIM, MXU_DIM, MXU_DIM)),
                  full((width // MXU_DIM, MXU_DIM, MXU_DIM)),
                  full((3, width, LANES)), full((1, LANES)),
                  full((1, width)), full((1, width))],
        out_specs=pl.BlockSpec((nb, L, width), lambda b, s: (b, s, 0)),
        out_shape=jax.ShapeDtypeStruct((batch, seq, width), BF16),
        scratch_shapes=[pltpu.VMEM((nb, SUBLANES, width), F32),
                        pltpu.VMEM((nb * MLSTM_HEADS, hd, hd), F32),
                        pltpu.VMEM((nb * MLSTM_HEADS, 1, hd), F32),
                        pltpu.VMEM((nb * MLSTM_HEADS, SUBLANES, LANES), F32)],
        compiler_params=_params(("arbitrary", "arbitrary")),
        name="l0_mlstm",
    )(u3, u3, p["cw"], p["cb"], p["wq"], p["wk"], p["wv"], p["wg"], p["bg"], p["skip"], p["gn"])
    return out.reshape(batch * seq, width)


def _out_kernel(*refs, nin):
    y_refs = refs[:nin]
    w_ref, g_ref, x_ref, o_ref = refs[nin:]
    acc = jnp.dot(y_refs[0][...], w_ref[0], preferred_element_type=F32)
    for n in range(1, nin):
        acc = acc + jnp.dot(y_refs[n][...], w_ref[n], preferred_element_type=F32)
    o_ref[...] = x_ref[...] + _rms_norm(acc, g_ref[...])


def _out_proj(ys, w_bf16, g, x2, tm=512):
    nin, kin, d = w_bf16.shape
    t = x2.shape[0]
    return pl.pallas_call(
        functools.partial(_out_kernel, nin=nin),
        grid=(t // tm,),
        in_specs=[pl.BlockSpec((tm, kin), lambda i: (i, 0)) for _ in range(nin)]
        + [pl.BlockSpec((nin, kin, d), lambda i: (0, 0, 0), pipeline_mode=pl.Buffered(1)),
           pl.BlockSpec((1, d), lambda i: (0, 0)),
           pl.BlockSpec((tm, d), lambda i: (i, 0))],
        out_specs=pl.BlockSpec((tm, d), lambda i: (i, 0)),
        out_shape=jax.ShapeDtypeStruct((t, d), F32),
        compiler_params=_params(("arbitrary",)),
        name="out_proj",
    )(*ys, w_bf16, g, x2)


def _shift_rows(x, last_ref, first):
    row = lax.broadcasted_iota(jnp.int32, x.shape, 0)
    last = jnp.where(first, 0.0, last_ref[...])
    prev = jnp.where(row == 0, last, pltpu.roll(x, 1, 0))
    last_ref[...] = x[x.shape[0] - 1:x.shape[0], :]
    return prev


def _l1_in_kernel(x_ref, g_ref, w_ref, muw_ref, mua_ref, w1_ref, a1_ref, mu_ref, o_ref,
                  lo_ref, h_ref, last_ref, lastu_ref, *, tiles_per_seq):
    i = pl.program_id(0)
    j = pl.program_id(1)
    first = i % tiles_per_seq == 0

    @pl.when(j == 0)
    def _():
        h = _rms_norm(x_ref[...], g_ref[...])
        h_ref[...] = h.astype(BF16)
        dh = _shift_rows(h, last_ref, first) - h
        xw = h + dh * muw_ref[...]
        xa = h + dh * mua_ref[...]
        lo_ref[:, 0:LORA_PAD] = jnp.tanh(_dot(xw, w1_ref[...]))
        lo_ref[:, LORA_PAD:2 * LORA_PAD] = _dot(xa, a1_ref[...])

    acc = jnp.dot(h_ref[...], w_ref[...], preferred_element_type=F32)
    o_ref[...] = acc + (_shift_rows(acc, lastu_ref.at[j], first) - acc) * mu_ref[...]


def _l1_in(x2, g, w_bf16, muw, mua, w1, a1, mu_rkv, seq, tm=1024, tn=1024):
    t, d = x2.shape
    n = w_bf16.shape[1]
    mu_rkv = _pad_cols(mu_rkv, n)
    c2 = lambda shape: pl.BlockSpec(shape, lambda i, j: (0, 0))
    return pl.pallas_call(
        functools.partial(_l1_in_kernel, tiles_per_seq=seq // tm),
        grid=(t // tm, n // tn),
        in_specs=[pl.BlockSpec((tm, d), lambda i, j: (i, 0)),
                  c2((1, d)),
                  pl.BlockSpec((d, tn), lambda i, j: (0, j)),
                  c2((1, d)), c2((1, d)), c2((d, LORA_PAD)), c2((d, LORA_PAD)),
                  pl.BlockSpec((1, tn), lambda i, j: (0, j))],
        out_specs=[pl.BlockSpec((tm, tn), lambda i, j: (i, j)),
                   pl.BlockSpec((tm, 2 * LORA_PAD), lambda i, j: (i, 0))],
        out_shape=[jax.ShapeDtypeStruct((t, n), F32),
                   jax.ShapeDtypeStruct((t, 2 * LORA_PAD), F32)],
        scratch_shapes=[pltpu.VMEM((tm, d), BF16), pltpu.VMEM((1, d), F32),
                        pltpu.VMEM((n // tn, 1, tn), F32)],
        compiler_params=_params(("arbitrary", "arbitrary")),
        name="l1_in",
    )(x2, g, w_bf16, muw, mua, w1, a1, mu_rkv)


def _bdot(a, b):
    return jnp.dot(a, b, preferred_element_type=F32)


def _rwkv_kernel(r_ref, k_ref, v_ref, z_ref, lo_ref, w0_ref, w2_ref,
                 a0_ref, a2_ref, kk_ref, ka_ref, rk_ref, gg_ref, gb_ref, o_ref, st_s):
    L = RWKV_CHUNK
    D = RWKV_HEAD_DIM
    QW = RWKV_QUAD_WIDTH
    HQ = QW // D
    NQ = r_ref.shape[1] // QW
    s = pl.program_id(2)

    @pl.when(s == 0)
    def _():
        st_s[...] = jnp.zeros_like(st_s)

    r = r_ref[...]
    k = k_ref[...]
    v = v_ref[...]

    lo = lo_ref[...]
    logw = -RWKV_DECAY_SCALE * _sigmoid(w0_ref[...] + _dot(lo[:, 0:LORA_PAD], w2_ref[...]))
    a = _sigmoid(a0_ref[...] + _dot(lo[:, LORA_PAD:2 * LORA_PAD], a2_ref[...]))

    seg = ((lax.broadcasted_iota(jnp.int32, (QW, QW), 0) // D)
           == (lax.broadcasted_iota(jnp.int32, (QW, QW), 1) // D))
    seg_b = seg.astype(BF16)
    qs = [slice(q * QW, (q + 1) * QW) for q in range(NQ)]

    def segsum(x):
        stacked = jnp.concatenate([x[:, sl] for sl in qs], axis=0).astype(BF16)
        res = _bdot(stacked, seg_b)
        return jnp.concatenate([res[q * L:(q + 1) * L] for q in range(NQ)], axis=1)

    kk = k * kk_ref[...]
    kk = kk * lax.rsqrt(jnp.maximum(segsum(kk * kk), 1e-24))
    k = k * (1.0 + (a - 1.0) * ka_ref[...])
    bonus = segsum(r * k * rk_ref[...]) * v

    tri_b = _tri(L).astype(BF16)
    lw_hi = logw.astype(BF16)
    lw_lo = (logw - lw_hi.astype(F32)).astype(BF16)
    cs = _bdot(tri_b, lw_hi) + _bdot(tri_b, lw_lo)
    p_in = jnp.exp(cs)
    p_inv = jnp.exp(-cs)
    p_ex = jnp.exp(cs - logw)
    p_last = p_in[L - 1:L, :]

    rt = r * p_in
    at = -kk * p_ex
    kt = k * p_inv
    bt = kk * a * p_inv

    per_tile = LANES // D
    lane = lax.broadcasted_iota(jnp.int32, (1, LANES), 1) // D
    hmb = [(lane == h).astype(BF16) for h in range(per_tile)]
    rowi = lax.broadcasted_iota(jnp.int32, (L, QW), 0)
    coli = lax.broadcasted_iota(jnp.int32, (L, QW), 1) % D
    strict = coli < rowi
    lower = coli <= rowi
    eye = (coli == rowi).astype(F32)
    zero_tile = jnp.zeros((L, LANES), BF16)

    def head_rows(mb):
        blocks = []
        for h in range(HQ):
            t = h // per_tile
            keep = mb[:, t * LANES:(t + 1) * LANES] * hmb[h % per_tile]
            blocks.append(jnp.concatenate(
                [keep if c == t else zero_tile for c in range(QW // LANES)], axis=1))
        return blocks

    def bd(m):
        return jnp.concatenate(head_rows(m.astype(BF16)), axis=0)

    def cat2(x, y):
        return jnp.concatenate([x, y], axis=0)

    xs, zs, vs, a_ab, a_ak, a_rb, a_rk = [], [], [], [], [], [], []
    for sl in qs:
        xq = cat2(at[:, sl], rt[:, sl]).astype(BF16)
        btb = bt[:, sl].astype(BF16)
        ktb = kt[:, sl].astype(BF16)
        ycat = jnp.concatenate(head_rows(btb) + head_rows(ktb), axis=0)
        aall = lax.dot_general(xq, ycat, (((1,), (1,)), ((), ())),
                               preferred_element_type=F32)
        xs.append(xq)
        zs.append(cat2(btb, ktb))
        vs.append(v[:, sl])
        a_ab.append(jnp.where(strict, aall[0:L, 0:QW], 0.0))
        a_ak.append(jnp.where(strict, aall[0:L, QW:2 * QW], 0.0))
        a_rb.append(jnp.where(lower, aall[L:2 * L, 0:QW], 0.0))
        a_rk.append(jnp.where(lower, aall[L:2 * L, QW:2 * QW], 0.0))

    def bdb(mb):
        return jnp.concatenate(head_rows(mb), axis=0)

    tinv = [eye + m for m in a_ab]
    pwb = [m.astype(BF16) for m in a_ab]
    pwb = [_bdot(pwb[q], bdb(pwb[q])).astype(BF16) for q in range(NQ)]
    sx = [_bdot(xs[q], st_s[q].astype(BF16)) for q in range(NQ)]
    av = [_bdot(cat2(a_ak[q], a_rk[q]).astype(BF16), bd(vs[q])) for q in range(NQ)]
    for _ in range(L.bit_length() - 3):
        both = [_bdot(cat2(tinv[q].astype(BF16), pwb[q]), bdb(pwb[q])) for q in range(NQ)]
        tinv = [tinv[q] + both[q][0:L] for q in range(NQ)]
        pwb = [both[q][L:2 * L].astype(BF16) for q in range(NQ)]
    tinv = [tinv[q] + _bdot(tinv[q].astype(BF16), bdb(pwb[q])) for q in range(NQ)]

    us = [_bdot(tinv[q].astype(BF16), bd(sx[q][0:L] + av[q][0:L])) for q in range(NQ)]
    ys = [sx[q][L:2 * L] + av[q][L:2 * L] + _bdot(a_rb[q].astype(BF16), bd(us[q]))
          for q in range(NQ)]

    for q, sl in enumerate(qs):
        upd = lax.dot_general(zs[q], cat2(us[q], vs[q]).astype(BF16),
                              (((0,), (0,)), ((), ())),
                              preferred_element_type=F32)
        cols = []
        for cb in range(QW // LANES):
            lo_l = q * QW + cb * LANES
            cols.append(jnp.broadcast_to(p_last[:, lo_l:lo_l + LANES], (LANES, LANES)).T)
        pcol = jnp.concatenate(cols, axis=0)
        pcol = jnp.concatenate([pcol] * (QW // LANES), axis=1)
        st_s[q] = pcol * (st_s[q] + jnp.where(seg, upd, 0.0))

    yv = jnp.concatenate(ys, axis=1)
    cen = yv - segsum(yv) * (1.0 / D)
    var = segsum(cen * cen) * (1.0 / D)
    y = cen * lax.rsqrt(var + RWKV_GN_EPS) * gg_ref[...] + gb_ref[...] + bonus
    o_ref[...] = (y * _silu(z_ref[...].astype(F32))).astype(BF16)


def _l1_rwkv(u, lo, p, batch, seq, width):
    L = RWKV_CHUNK
    QW = RWKV_QUAD_WIDTH
    wt = RWKV_QUADS_PER_STEP * QW
    ngrp = width // wt
    nst = seq // L
    vec = lambda: pl.BlockSpec((1, wt), lambda b, g, s: (0, g))
    col = lambda off: pl.BlockSpec((L, wt), lambda b, g, s: (b * nst + s, off + g))
    return pl.pallas_call(
        _rwkv_kernel,
        grid=(batch, ngrp, nst),
        in_specs=[col(0), col(ngrp), col(2 * ngrp), col(3 * ngrp),
                  pl.BlockSpec((L, 2 * LORA_PAD), lambda b, g, s: (b * nst + s, 0)),
                  vec(),
                  pl.BlockSpec((LORA_PAD, wt), lambda b, g, s: (0, g)),
                  vec(),
                  pl.BlockSpec((LORA_PAD, wt), lambda b, g, s: (0, g)),
                  vec(), vec(), vec(), vec(), vec()],
        out_specs=pl.BlockSpec((L, wt), lambda b, g, s: (b * nst + s, g)),
        out_shape=jax.ShapeDtypeStruct((batch * seq, width), BF16),
        scratch_shapes=[pltpu.VMEM((RWKV_QUADS_PER_STEP, QW, QW), F32)],
        compiler_params=_params(("arbitrary", "arbitrary", "arbitrary")),
        name="l1_rwkv",
    )(u, u, u, u, lo, p["w0"], p["w2"], p["a0"], p["a2"],
      p["kk"], p["ka"], p["rk"], p["gg"], p["gb"])


def _bd4_dense(w):
    nb, bs, _ = w.shape
    per = MXU_DIM // bs
    rep = jnp.tile(w.reshape(nb // per, MXU_DIM, bs), (1, 1, per))
    idx = np.arange(MXU_DIM) // bs
    keep = jnp.asarray(idx[:, None] == idx[None, :])
    return jnp.where(keep, rep, 0.0).astype(BF16)


def _row(v):
    return v.reshape(1, -1).astype(F32)


def _pad_rows(w, rows):
    return jnp.pad(w, ((0, rows - w.shape[0]), (0, 0)))


def _pad_cols(w, cols):
    return jnp.pad(w, ((0, 0), (0, cols - w.shape[1])))


def kernel(x, l0_norm_pre, l0_w_in, l0_lru_conv_w, l0_lru_conv_b, l0_lru_wa, l0_lru_ba,
           l0_lru_wx, l0_lru_bx, l0_lru_lambda, l0_m_conv_w, l0_m_conv_b, l0_m_wq, l0_m_wk,
           l0_m_wv, l0_m_wi, l0_m_bi, l0_m_wf, l0_m_bf, l0_m_skip, l0_m_gn, l0_w_out,
           l0_norm_post, l1_norm_pre, l1_w_in, l1_mu_rkv, l1_mu_w, l1_mu_a, l1_w0, l1_w1,
           l1_w2, l1_a0, l1_a1, l1_a2, l1_k_k, l1_k_a, l1_r_k, l1_gn_g, l1_gn_b, l1_w_out,
           l1_norm_post):
    batch, seq, d = x.shape
    x2 = x.reshape(batch * seq, d).astype(F32)
    rw = l0_lru_lambda.shape[0]
    mw = l0_m_skip.shape[0]
    ww = l1_w0.shape[0]
    assert rw == mw, "column blocking assumes equal head-group widths"

    u0 = _l0_in(x2, _row(l0_norm_pre), l0_w_in.astype(BF16))

    lru_p = dict(cw=l0_lru_conv_w.astype(F32), cb=_row(l0_lru_conv_b),
                 wa=l0_lru_wa.astype(BF16), ba=_row(l0_lru_ba),
                 wx=l0_lru_wx.astype(BF16), bx=_row(l0_lru_bx), lam=_row(l0_lru_lambda))
    yr = _l0_lru(u0, lru_p, batch, seq, rw)

    wg = jnp.concatenate([l0_m_wi, l0_m_wf], axis=1)
    wg = _pad_cols(wg, LANES).reshape(3, mw, LANES).astype(BF16)
    bg = _pad_cols(jnp.concatenate([l0_m_bi, l0_m_bf]).reshape(1, -1), LANES).astype(F32)
    m_p = dict(cw=l0_m_conv_w.astype(F32), cb=_row(l0_m_conv_b),
               wq=_bd4_dense(l0_m_wq), wk=_bd4_dense(l0_m_wk),
               wv=_bd4_dense(l0_m_wv), wg=wg, bg=bg,
               skip=_row(l0_m_skip), gn=_row(l0_m_gn))
    ym = _l0_mlstm(u0, m_p, batch, seq, mw, x_col=2 * rw // mw, z_col=(2 * rw + mw) // mw)

    w_out0 = l0_w_out.astype(BF16).reshape(2, rw, d)
    x1 = _out_proj([yr, ym], w_out0, _row(l0_norm_post), x2)

    w1 = _pad_cols(l1_w1, LORA_PAD).astype(BF16)
    a1 = _pad_cols(l1_a1, LORA_PAD).astype(BF16)
    u1, lo1 = _l1_in(x1, _row(l1_norm_pre), l1_w_in.astype(BF16), _row(l1_mu_w),
                           _row(l1_mu_a), w1, a1, _row(l1_mu_rkv), seq)
    r_p = dict(w0=_row(l1_w0),
               w2=_pad_rows(l1_w2, LORA_PAD).astype(BF16), a0=_row(l1_a0),
               a2=_pad_rows(l1_a2, LORA_PAD).astype(BF16), kk=_row(l1_k_k), ka=_row(l1_k_a),
               rk=_row(l1_r_k), gg=_row(l1_gn_g), gb=_row(l1_gn_b))
    y1 = _l1_rwkv(u1, lo1, r_p, batch, seq, ww)
    out = _out_proj([y1], l1_w_out.astype(BF16).reshape(1, ww, d), _row(l1_norm_post), x1)
    return out.reshape(batch, seq, d).astype(x.dtype)
```
